```python
import jax, jax.numpy as jnp
from jax import lax
import numpy as np

D_MODEL = 1024
BATCH = 8
SEQ = 4096
DEPTH = 2
DEC_BATCH = 8
DEC_SEQ = 64
PAST_LEN = 4096

CHUNK = 64
N_MIXERS = 2
N_POOL_LAYERS = (DEPTH + 1) // 2
N_CONV_LAYERS = DEPTH // 2
POOL_WINDOWS = (2, 4, 8, 16)
N_POOL_GROUPS = 4
GROUP_WIDTH = D_MODEL // N_POOL_GROUPS
POOL_HIST = max(POOL_WINDOWS) - 1
CONV_WIDTH = 31
CONV_HIST = CONV_WIDTH - 1
PEER_HEADS = 8
PEER_NKEYS = 128
PEER_EXPERTS = PEER_NKEYS * PEER_NKEYS
PEER_DKEY = 128
PEER_TOPK = 16
PEER_BLOCK = 256
EPS = 1e-6

kernel_name = 'streaming_pool_conformer_peer'


def rmsnorm(x, g):
    xf = x.astype(jnp.float32)
    y = xf * lax.rsqrt(jnp.mean(xf * xf, axis=-1, keepdims=True) + EPS)
    return (y * g.astype(jnp.float32)).astype(x.dtype)


def pool_mixer(xn, hist, pos0, w, scale):
    B, S, D = xn.shape
    xp = jnp.concatenate([hist, xn], axis=1)
    c = jnp.cumsum(xp.astype(jnp.float32), axis=1)
    c = jnp.pad(c, ((0, 0), (1, 0), (0, 0)))
    end = c[:, POOL_HIST + 1:]
    pos = pos0 + jnp.arange(S)
    diffs = []
    for g, win in enumerate(POOL_WINDOWS):
        sl = slice(g * GROUP_WIDTH, (g + 1) * GROUP_WIDTH)
        start = c[:, POOL_HIST + 1 - win: POOL_HIST + 1 - win + S, sl]
        cnt = jnp.minimum(win, pos + 1).astype(jnp.float32)[None, :, None]
        mean = (end[..., sl] - start) / cnt
        diffs.append(mean - xn[..., sl].astype(jnp.float32))
    d = jnp.stack(diffs, axis=2).astype(xn.dtype)
    y = jnp.einsum('bsgc,gce->bsge', d, w).reshape(B, S, D) * scale
    return y, xp[:, -POOL_HIST:]


def conv_module(xn, hist, w_in, b_in, dw, dw_b, ln_g, ln_b, w_out, b_out):
    D = xn.shape[-1]
    h = xn @ w_in + b_in
    a, gate = jnp.split(h, 2, axis=-1)
    u = a * jax.nn.sigmoid(gate)
    up = jnp.concatenate([hist, u], axis=1)
    z = lax.conv_general_dilated(up, dw[:, None, :], window_strides=(1,), padding='VALID',
                                 dimension_numbers=('NWC', 'WIO', 'NWC'),
                                 feature_group_count=D) + dw_b
    zf = z.astype(jnp.float32)
    mu = jnp.mean(zf, axis=-1, keepdims=True)
    var = jnp.mean(jnp.square(zf - mu), axis=-1, keepdims=True)
    zn = ((zf - mu) * lax.rsqrt(var + EPS) * ln_g.astype(jnp.float32) + ln_b.astype(jnp.float32)).astype(xn.dtype)
    y = jax.nn.silu(zn) @ w_out + b_out
    return y, up[:, -CONV_HIST:]


def peer(xn, wq, keys, u_tab, v_tab):
    B, S, D = xn.shape
    T = B * S
    nblk = -(-T // PEER_BLOCK)
    xt = jnp.pad(xn.reshape(T, D), ((0, nblk * PEER_BLOCK - T), (0, 0)))
    half = PEER_DKEY // 2

    def block(xb):
        q = (xb @ wq).reshape(PEER_BLOCK, PEER_HEADS, PEER_DKEY)
        s1 = jnp.einsum('thc,hnc->thn', q[..., :half], keys[:, 0])
        s2 = jnp.einsum('thc,hnc->thn', q[..., half:], keys[:, 1])
        v1, i1 = lax.top_k(s1, PEER_TOPK)
        v2, i2 = lax.top_k(s2, PEER_TOPK)
        comb = (v1[..., :, None] + v2[..., None, :]).reshape(PEER_BLOCK, PEER_HEADS, PEER_TOPK * PEER_TOPK)
        vals, p = lax.top_k(comb, PEER_TOPK)
        e = (jnp.take_along_axis(i1, p // PEER_TOPK, axis=-1) * PEER_NKEYS
             + jnp.take_along_axis(i2, p % PEER_TOPK, axis=-1))
        g = jax.nn.softmax(vals.astype(jnp.float32), axis=-1).astype(xb.dtype)
        hid = jax.nn.gelu(jnp.einsum('td,thkd->thk', xb, u_tab[e]), approximate=False)
        return jnp.einsum('thk,thkd->td', g * hid, v_tab[e])

    y = lax.map(block, xt.reshape(nblk, PEER_BLOCK, D))
    return y.reshape(nblk * PEER_BLOCK, D)[:T].reshape(B, S, D)


def trunk(x, pool_hist, conv_hist, pos0, norm_mix, norm_ffn, norm_final, pool_w, pool_scale,
          conv_w_in, conv_b_in, conv_dw, conv_dw_b, conv_ln_g, conv_ln_b, conv_w_out, conv_b_out,
          peer_wq, peer_keys, peer_u, peer_v):
    new_pool, new_conv = [], []
    for i in range(DEPTH):
        h = rmsnorm(x, norm_mix[i])
        j = i // N_MIXERS
        if i % N_MIXERS == 0:
            y, st = pool_mixer(h, pool_hist[j], pos0, pool_w[j], pool_scale[j])
            new_pool.append(st)
        else:
            y, st = conv_module(h, conv_hist[j], conv_w_in[j], conv_b_in[j], conv_dw[j], conv_dw_b[j],
                                conv_ln_g[j], conv_ln_b[j], conv_w_out[j], conv_b_out[j])
            new_conv.append(st)
        x = x + y
        x = x + peer(rmsnorm(x, norm_ffn[i]), peer_wq[i], peer_keys[i], peer_u[i], peer_v[i])
    return rmsnorm(x, norm_final), jnp.stack(new_pool), jnp.stack(new_conv)


def setup_inputs(seed: int = 0) -> dict:
    key = jax.random.key(seed)
    ks = jax.random.split(key, 24)
    D = D_MODEL
    f = jnp.float32
    nrm = lambda k, shape, s: jax.random.normal(k, shape, f) * s
    return {
        'x_prompt': nrm(ks[0], (BATCH, SEQ, D), 1.0),
        'x_sample': nrm(ks[1], (DEC_BATCH, DEC_SEQ, D), 1.0),
        'cache_pool': nrm(ks[2], (N_POOL_LAYERS, DEC_BATCH, POOL_HIST, D), 1.0),
        'state_conv': nrm(ks[3], (N_CONV_LAYERS, DEC_BATCH, CONV_HIST, D), 0.5),
        'norm_mix': 1.0 + nrm(ks[4], (DEPTH, D), 0.02),
        'norm_ffn': 1.0 + nrm(ks[5], (DEPTH, D), 0.02),
        'norm_final': 1.0 + nrm(ks[6], (D,), 0.02),
        'pool_w': nrm(ks[7], (N_POOL_LAYERS, N_POOL_GROUPS, GROUP_WIDTH, GROUP_WIDTH), GROUP_WIDTH ** -0.5),
        'pool_scale': 1.0 + nrm(ks[8], (N_POOL_LAYERS, D), 0.02),
        'conv_w_in': nrm(ks[9], (N_CONV_LAYERS, D, 2 * D), D ** -0.5),
        'conv_b_in': nrm(ks[10], (N_CONV_LAYERS, 2 * D), 0.01),
        'conv_dw': nrm(ks[11], (N_CONV_LAYERS, CONV_WIDTH, D), CONV_WIDTH ** -0.5),
        'conv_dw_b': nrm(ks[12], (N_CONV_LAYERS, D), 0.01),
        'conv_ln_g': 1.0 + nrm(ks[13], (N_CONV_LAYERS, D), 0.02),
        'conv_ln_b': nrm(ks[14], (N_CONV_LAYERS, D), 0.01),
        'conv_w_out': nrm(ks[15], (N_CONV_LAYERS, D, D), D ** -0.5),
        'conv_b_out': nrm(ks[16], (N_CONV_LAYERS, D), 0.01),
        'peer_wq': nrm(ks[17], (DEPTH, D, PEER_HEADS * PEER_DKEY), D ** -0.5),
        'peer_keys': nrm(ks[18], (DEPTH, PEER_HEADS, 2, PEER_NKEYS, PEER_DKEY // 2), (PEER_DKEY // 2) ** -0.5),
        'peer_u': nrm(ks[19], (DEPTH, PEER_EXPERTS, D), D ** -0.5),
        'peer_v': nrm(ks[20], (DEPTH, PEER_EXPERTS, D), 0.25),
    }


def reference(x_prompt, x_sample, cache_pool, state_conv, norm_mix, norm_ffn, norm_final, pool_w, pool_scale,
              conv_w_in, conv_b_in, conv_dw, conv_dw_b, conv_ln_g, conv_ln_b, conv_w_out, conv_b_out,
              peer_wq, peer_keys, peer_u, peer_v):
    B, _, D = x_prompt.shape
    zero_pool = jnp.zeros((N_POOL_LAYERS, B, POOL_HIST, D), x_prompt.dtype)
    zero_conv = jnp.zeros((N_CONV_LAYERS, B, CONV_HIST, D), x_prompt.dtype)
    y_prompt, new_pool_prompt, new_conv_prompt = trunk(
        x_prompt, zero_pool, zero_conv, 0, norm_mix, norm_ffn, norm_final, pool_w, pool_scale,
        conv_w_in, conv_b_in, conv_dw, conv_dw_b, conv_ln_g, conv_ln_b, conv_w_out, conv_b_out,
        peer_wq, peer_keys, peer_u, peer_v)
    y_sample, new_pool_sample, new_conv_sample = trunk(
        x_sample, cache_pool, state_conv, PAST_LEN, norm_mix, norm_ffn, norm_final, pool_w, pool_scale,
        conv_w_in, conv_b_in, conv_dw, conv_dw_b, conv_ln_g, conv_ln_b, conv_w_out, conv_b_out,
        peer_wq, peer_keys, peer_u, peer_v)
    return (y_prompt, y_sample, new_pool_prompt, new_conv_prompt, new_pool_sample, new_conv_sample)
```

```python
import functools
import math

import jax
import jax.numpy as jnp
from jax import lax
from jax.experimental import pallas as pl
from jax.experimental.pallas import tpu as pltpu

EPS = 1e-6
POOL_WINDOWS = (2, 4, 8, 16)
POOL_HIST = 15
POOL_PAD = 16
CONV_WIDTH = 31
CONV_HIST = 30
CONV_PAD = 32
PEER_HEADS = 8
PEER_NKEYS = 128
PEER_HALF = 64
PEER_TOPK = 16
LANES = 128
VMEM_LIMIT = 56 * 1024 * 1024

F32 = jnp.float32
BF16 = jnp.bfloat16
NT_DIMS = (((1,), (1,)), ((), ()))


def _rmsnorm(x, g):
    ms = jnp.mean(x * x, axis=-1, keepdims=True)
    return x * lax.rsqrt(ms + EPS) * g


def _split_bf16(a):
    hi = a.astype(BF16)
    lo = (a - hi.astype(F32)).astype(BF16)
    return hi, lo


def _dot(a, b):
    return jnp.dot(a, b, preferred_element_type=F32)


def _dot3(a_hi, a_lo, b_hi, b_lo):
    return _dot(a_hi, b_hi) + _dot(a_lo, b_hi) + _dot(a_hi, b_lo)


def _dot3_nt(a_hi, a_lo, b_hi, b_lo):
    dg = functools.partial(lax.dot_general, dimension_numbers=NT_DIMS, preferred_element_type=F32)
    return dg(a_hi, b_hi) + dg(a_lo, b_hi) + dg(a_hi, b_lo)


def _pool_kernel(x_ref, hist_ref, g_ref, wh_ref, wl_ref, sc_ref, o_ref, np_ref, xp_ref, *, ts, pos0, n_s, gw):
    s = pl.program_id(1)

    @pl.when(s == 0)
    def _():
        xp_ref[0:POOL_PAD, :] = hist_ref[0]

    @pl.when(s > 0)
    def _():
        xp_ref[0:POOL_PAD, :] = xp_ref[ts:ts + POOL_PAD, :]

    x = x_ref[0]
    xn = _rmsnorm(x, g_ref[...])
    xp_ref[POOL_PAD:POOL_PAD + ts, :] = xn

    for g, win in enumerate(POOL_WINDOWS):
        c0 = g * gw
        xg = xn[:, c0:c0 + gw]
        acc = xg
        for j in range(1, win):
            acc = acc + xp_ref[POOL_PAD - j:POOL_PAD - j + ts, c0:c0 + gw]
        if pos0 >= win - 1:
            mean = acc * (1.0 / win)
        else:
            pos = pos0 + s * ts + lax.broadcasted_iota(jnp.int32, (ts, gw), 0)
            cnt = jnp.minimum(win, pos + 1).astype(F32)
            mean = acc / cnt
        d_hi, d_lo = _split_bf16(mean - xg)
        y = _dot3(d_hi, d_lo, wh_ref[g], wl_ref[g])
        o_ref[0, :, c0:c0 + gw] = x[:, c0:c0 + gw] + y * sc_ref[:, c0:c0 + gw]

    @pl.when(s == n_s - 1)
    def _():
        np_ref[0] = xp_ref[ts:ts + POOL_PAD, :]


def _pool_layer(x, hist, pos0, g, w_hi, w_lo, scale):
    B, S, D = x.shape
    ts = min(S, 512)
    n_s = S // ts
    gw = D // len(POOL_WINDOWS)
    hist_p = jnp.pad(hist, ((0, 0), (POOL_PAD - POOL_HIST, 0), (0, 0)))
    kern = functools.partial(_pool_kernel, ts=ts, pos0=pos0, n_s=n_s, gw=gw)
    out, new_hist = pl.pallas_call(
        kern,
        grid=(B, n_s),
        in_specs=[
            pl.BlockSpec((1, ts, D), lambda b, s: (b, s, 0)),
            pl.BlockSpec((1, POOL_PAD, D), lambda b, s: (b, 0, 0)),
            pl.BlockSpec((1, D), lambda b, s: (0, 0)),
            pl.BlockSpec(w_hi.shape, lambda b, s: (0, 0, 0)),
            pl.BlockSpec(w_lo.shape, lambda b, s: (0, 0, 0)),
            pl.BlockSpec((1, D), lambda b, s: (0, 0)),
        ],
        out_specs=[
            pl.BlockSpec((1, ts, D), lambda b, s: (b, s, 0)),
            pl.BlockSpec((1, POOL_PAD, D), lambda b, s: (b, 0, 0)),
        ],
        out_shape=[
            jax.ShapeDtypeStruct((B, S, D), F32),
            jax.ShapeDtypeStruct((B, POOL_PAD, D), F32),
        ],
        scratch_shapes=[pltpu.VMEM((POOL_PAD + ts, D), F32)],
        compiler_params=pltpu.CompilerParams(
            dimension_semantics=("arbitrary", "arbitrary"), vmem_limit_bytes=VMEM_LIMIT),
        name="pool_layer",
    )(x, hist_p, g.reshape(1, D), w_hi, w_lo, scale.reshape(1, D))
    return out, new_hist[:, POOL_PAD - POOL_HIST:]


def _conv_kernel(x_ref, hist_ref, g_ref, win_ref, bin_ref, dw_ref, dwb_ref, lng_ref, lnb_ref, wout_ref, bout_ref,
                 o_ref, nc_ref, up_ref, z_ref, *, ts, n_s, rb):
    s = pl.program_id(1)
    D = x_ref.shape[-1]

    @pl.when(s == 0)
    def _():
        up_ref[0:CONV_PAD, :] = hist_ref[0]

    @pl.when(s > 0)
    def _():
        up_ref[0:CONV_PAD, :] = up_ref[ts:ts + CONV_PAD, :]

    x = x_ref[0]
    xn = _rmsnorm(x, g_ref[...])
    h = _dot(xn.astype(BF16), win_ref[...]) + bin_ref[...]
    u = h[:, :D] * jax.nn.sigmoid(h[:, D:])
    up_ref[CONV_PAD:CONV_PAD + ts, :] = u

    base = CONV_PAD - CONV_HIST
    for r0 in range(0, ts, rb):
        for c0 in range(0, D, LANES):
            acc = jnp.zeros((rb, LANES), F32) + dwb_ref[:, c0:c0 + LANES]
            for k in range(CONV_WIDTH):
                acc = acc + up_ref[base + r0 + k:base + r0 + k + rb, c0:c0 + LANES] * dw_ref[k:k + 1, c0:c0 + LANES]
            z_ref[r0:r0 + rb, c0:c0 + LANES] = acc

    z = z_ref[...]
    mu = jnp.mean(z, axis=-1, keepdims=True)
    zc = z - mu
    var = jnp.mean(zc * zc, axis=-1, keepdims=True)
    zn = zc * lax.rsqrt(var + EPS) * lng_ref[...] + lnb_ref[...]
    act = zn * jax.nn.sigmoid(zn)
    y = _dot(act.astype(BF16), wout_ref[...]) + bout_ref[...]
    o_ref[0] = x + y

    @pl.when(s == n_s - 1)
    def _():
        nc_ref[0] = up_ref[ts:ts + CONV_PAD, :]


def _conv_layer(x, hist, g, w_in, b_in, dw, dw_b, ln_g, ln_b, w_out, b_out):
    B, S, D = x.shape
    ts = min(S, 256)
    n_s = S // ts
    hist_p = jnp.pad(hist, ((0, 0), (CONV_PAD - CONV_HIST, 0), (0, 0)))
    kern = functools.partial(_conv_kernel, ts=ts, n_s=n_s, rb=min(ts, 64))
    row = lambda a: a.reshape(1, -1)
    const2 = lambda b, s: (0, 0)
    out, new_hist = pl.pallas_call(
        kern,
        grid=(B, n_s),
        in_specs=[
            pl.BlockSpec((1, ts, D), lambda b, s: (b, s, 0)),
            pl.BlockSpec((1, CONV_PAD, D), lambda b, s: (b, 0, 0)),
            pl.BlockSpec((1, D), const2),
            pl.BlockSpec((D, 2 * D), const2),
            pl.BlockSpec((1, 2 * D), const2),
            pl.BlockSpec((CONV_WIDTH, D), const2),
            pl.BlockSpec((1, D), const2),
            pl.BlockSpec((1, D), const2),
            pl.BlockSpec((1, D), const2),
            pl.BlockSpec((D, D), const2),
            pl.BlockSpec((1, D), const2),
        ],
        out_specs=[
            pl.BlockSpec((1, ts, D), lambda b, s: (b, s, 0)),
            pl.BlockSpec((1, CONV_PAD, D), lambda b, s: (b, 0, 0)),
        ],
        out_shape=[
            jax.ShapeDtypeStruct((B, S, D), F32),
            jax.ShapeDtypeStruct((B, CONV_PAD, D), F32),
        ],
        scratch_shapes=[pltpu.VMEM((CONV_PAD + ts, D), F32), pltpu.VMEM((ts, D), F32)],
        compiler_params=pltpu.CompilerParams(
            dimension_semantics=("arbitrary", "arbitrary"), vmem_limit_bytes=VMEM_LIMIT),
        name="conv_layer",
    )(x, hist_p, row(g), w_in.astype(BF16), row(b_in), dw, row(dw_b), row(ln_g), row(ln_b),
      w_out.astype(BF16), row(b_out))
    return out, new_hist[:, CONV_PAD - CONV_HIST:]


def _top16(s, v_ref):
    work = s
    rank = jnp.full(s.shape, float(PEER_TOPK), F32)
    for r in range(PEER_TOPK):
        m = jnp.max(work, axis=0, keepdims=True)
        v_ref[r:r + 1, :] = m
        hit = work == m
        rank = jnp.where(hit, float(r), rank)
        work = jnp.where(hit, -jnp.inf, work)
    return rank


def _pair_select(v1, v2):
    row = lax.broadcasted_iota(jnp.int32, (8, LANES), 0)
    neg = -jnp.inf
    cands = [v1[0:1] + v2[0:8], v1[0:1] + v2[8:16], v1[1:2] + v2[0:8]]
    for a in range(2, 8):
        cands.append(jnp.where(row < PEER_TOPK // (a + 1), v1[a:a + 1] + v2[0:8], neg))
    cands.append(v1[8:16] + v2[0:1])

    work = list(cands)
    vals = []
    for _ in range(PEER_TOPK):
        m8 = work[0]
        for w in work[1:]:
            m8 = jnp.maximum(m8, w)
        m = jnp.max(m8, axis=0, keepdims=True)
        vals.append(m)
        work = [jnp.where(w == m, neg, w) for w in work]
    tau, top = vals[-1], vals[0]
    z = jnp.ones_like(top)
    for v in vals[1:]:
        z = z + jnp.exp(v - top)

    sel = [(c >= tau).astype(F32) for c in cands]
    colsum = lambda a: jnp.sum(a, axis=0, keepdims=True)
    n_lo = [colsum(sel[0]) + colsum(sel[1])] + [colsum(sel[i]) for i in range(2, 9)]
    n_hi = sel[9]
    return n_lo, n_hi, z


def _route_kernel(x_ref, g_ref, wqh_ref, wql_ref, k1h_ref, k1l_ref, k2h_ref, k2l_ref,
                  xn_ref, r2_ref, e2_ref, n1_ref, e1_ref, qt_ref, s_ref, v1_ref, v2_ref, *, tt):
    xn = _rmsnorm(x_ref[...], g_ref[...])
    x_hi, x_lo = _split_bf16(xn)
    xn_ref[...] = x_hi
    qt_ref[...] = _dot3_nt(wqh_ref[...], wql_ref[...], x_hi, x_lo)

    def head_body(h, carry):
        q0 = pl.multiple_of(h * (2 * PEER_HALF), 2 * PEER_HALF)
        q1_hi, q1_lo = _split_bf16(qt_ref[pl.ds(q0, PEER_HALF), :])
        q2_hi, q2_lo = _split_bf16(qt_ref[pl.ds(q0 + PEER_HALF, PEER_HALF), :])
        s_ref[0] = _dot3(k1h_ref[h], k1l_ref[h], q1_hi, q1_lo)
        s_ref[1] = _dot3(k2h_ref[h], k2l_ref[h], q2_hi, q2_lo)
        for c0 in range(0, tt, LANES):
            cols = slice(c0, c0 + LANES)
            rank1 = _top16(s_ref[0, :, cols], v1_ref)
            rank2 = _top16(s_ref[1, :, cols], v2_ref)
            r2_ref[h, :, cols] = rank2
            v1 = v1_ref[...]
            v2 = v2_ref[...]
            e2_ref[h, :, cols] = jnp.exp(s_ref[1, :, cols] - v2[0:1])
            n_lo, n_hi, z = _pair_select(v1, v2)
            n1 = jnp.zeros((PEER_NKEYS, LANES), F32)
            for a in range(PEER_TOPK):
                n_a = n_lo[a] if a < 8 else n_hi[a - 8:a - 7]
                n1 = jnp.where(rank1 == float(a), n_a, n1)
            n1_ref[h, :, cols] = n1
            e1_ref[h, :, cols] = jnp.exp(s_ref[0, :, cols] - v1[0:1]) * (1.0 / z)
        return carry

    lax.fori_loop(0, PEER_HEADS, head_body, 0)


def _peer_route(x, g, wq_hi, wq_lo, k1_hi, k1_lo, k2_hi, k2_lo):
    T, D = x.shape
    tt = min(T, 512)
    hd = wq_hi.shape[0]
    kern = functools.partial(_route_kernel, tt=tt)
    c2 = lambda i: (0, 0)
    c3 = lambda i: (0, 0, 0)
    tab = jax.ShapeDtypeStruct((PEER_HEADS, PEER_NKEYS, T), F32)
    tab_spec = pl.BlockSpec((PEER_HEADS, PEER_NKEYS, tt), lambda i: (0, 0, i))
    return pl.pallas_call(
        kern,
        grid=(T // tt,),
        in_specs=[
            pl.BlockSpec((tt, D), lambda i: (i, 0)),
            pl.BlockSpec((1, D), c2),
            pl.BlockSpec((hd, D), c2),
            pl.BlockSpec((hd, D), c2),
            pl.BlockSpec(k1_hi.shape, c3),
            pl.BlockSpec(k1_hi.shape, c3),
            pl.BlockSpec(k1_hi.shape, c3),
            pl.BlockSpec(k1_hi.shape, c3),
        ],
        out_specs=[pl.BlockSpec((tt, D), lambda i: (i, 0)), tab_spec, tab_spec, tab_spec, tab_spec],
        out_shape=[jax.ShapeDtypeStruct((T, D), BF16), tab, tab, tab, tab],
        scratch_shapes=[
            pltpu.VMEM((hd, tt), F32),
            pltpu.VMEM((2, PEER_NKEYS, tt), F32),
            pltpu.VMEM((PEER_TOPK, LANES), F32),
            pltpu.VMEM((PEER_TOPK, LANES), F32),
        ],
        compiler_params=pltpu.CompilerParams(
            dimension_semantics=("arbitrary",), vmem_limit_bytes=VMEM_LIMIT),
        name="peer_route",
    )(x, g.reshape(1, D), wq_hi, wq_lo, k1_hi, k1_lo, k2_hi, k2_lo)


def _gelu(h):
    return 0.5 * h * (1.0 + lax.erf(h * (1.0 / math.sqrt(2.0))))


def _expert_kernel(xn_ref, x_ref, u_ref, vt_ref, r2_ref, e2_ref, n1_ref, e1_ref, gf_ref,
                   o_ref, acc_ref, hid_ref, w_ref, *, tt, ec, n_j, final):
    j = pl.program_id(1)

    @pl.when(j == 0)
    def _():
        acc_ref[...] = jnp.zeros_like(acc_ref)

    hid_ref[...] = lax.dot_general(u_ref[...], xn_ref[...], NT_DIMS, preferred_element_type=F32)

    for k in range(ec // PEER_NKEYS):
        rows = slice(k * PEER_NKEYS, (k + 1) * PEER_NKEYS)
        for c0 in range(0, tt, LANES):
            cols = slice(c0, c0 + LANES)
            gate = jnp.zeros((PEER_NKEYS, LANES), F32)
            for h in range(PEER_HEADS):
                n1 = n1_ref[h, 0, k:k + 1, cols]
                e1 = e1_ref[h, 0, k:k + 1, cols]
                gate = gate + jnp.where(r2_ref[h, :, cols] < n1, e1 * e2_ref[h, :, cols], 0.0)
            w_ref[rows, cols] = (gate * _gelu(hid_ref[rows, cols])).astype(BF16)

    acc_ref[...] += _dot(vt_ref[...], w_ref[...])

    @pl.when(j == n_j - 1)
    def _():
        out = x_ref[...] + acc_ref[...].T
        if final:
            out = _rmsnorm(out, gf_ref[...])
        o_ref[...] = out


def _peer_experts(x, xn, u_bf, vt_bf, r2, e2, n1, e1, g_final, final):
    T, D = x.shape
    n_exp = u_bf.shape[0]
    tt = min(T, 512)
    ec = 1024
    n_j = n_exp // ec
    i1c = ec // PEER_NKEYS
    n1 = n1.reshape(PEER_HEADS, PEER_NKEYS // i1c, i1c, T)
    e1 = e1.reshape(PEER_HEADS, PEER_NKEYS // i1c, i1c, T)
    kern = functools.partial(_expert_kernel, tt=tt, ec=ec, n_j=n_j, final=final)
    tab_spec = pl.BlockSpec((PEER_HEADS, PEER_NKEYS, tt), lambda i, j: (0, 0, i))
    row_spec = pl.BlockSpec((PEER_HEADS, 1, i1c, tt), lambda i, j: (0, j, 0, i))
    return pl.pallas_call(
        kern,
        grid=(T // tt, n_j),
        in_specs=[
            pl.BlockSpec((tt, D), lambda i, j: (i, 0)),
            pl.BlockSpec((tt, D), lambda i, j: (i, 0)),
            pl.BlockSpec((ec, D), lambda i, j: (j, 0)),
            pl.BlockSpec((D, ec), lambda i, j: (0, j)),
            tab_spec, tab_spec, row_spec, row_spec,
            pl.BlockSpec((1, D), lambda i, j: (0, 0)),
        ],
        out_specs=pl.BlockSpec((tt, D), lambda i, j: (i, 0)),
        out_shape=jax.ShapeDtypeStruct((T, D), F32),
        scratch_shapes=[
            pltpu.VMEM((D, tt), F32),
            pltpu.VMEM((ec, tt), F32),
            pltpu.VMEM((ec, tt), BF16),
        ],
        compiler_params=pltpu.CompilerParams(
            dimension_semantics=("arbitrary", "arbitrary"), vmem_limit_bytes=VMEM_LIMIT),
        name="peer_experts",
    )(xn, x, u_bf, vt_bf, r2, e2, n1, e1, g_final.reshape(1, D))


def _split_param(a):
    hi = a.astype(BF16)
    return hi, (a - hi.astype(F32)).astype(BF16)


def _trunk(x, pool_hist, conv_hist, pos0, p):
    B, S, D = x.shape
    x, new_pool = _pool_layer(x, pool_hist, pos0, p["norm_mix"][0], p["pool_w_hi"], p["pool_w_lo"], p["pool_scale"][0])
    for i in range(2):
        if i == 1:
            x, new_conv = _conv_layer(x, conv_hist, p["norm_mix"][1], p["conv_w_in"][0], p["conv_b_in"][0],
                                      p["conv_dw"][0], p["conv_dw_b"][0], p["conv_ln_g"][0], p["conv_ln_b"][0],
                                      p["conv_w_out"][0], p["conv_b_out"][0])
        xt = x.reshape(B * S, D)
        xn, r2, e2, n1, e1 = _peer_route(xt, p["norm_ffn"][i], *p["route"][i])
        xt = _peer_experts(xt, xn, p["u_bf"][i], p["vt_bf"][i], r2, e2, n1, e1, p["norm_final"], final=(i == 1))
        x = xt.reshape(B, S, D)
    return x, new_pool[None], new_conv[None]


def kernel(x_prompt, x_sample, cache_pool, state_conv, norm_mix, norm_ffn, norm_final, pool_w, pool_scale, conv_w_in, conv_b_in, conv_dw, conv_dw_b, conv_ln_g, conv_ln_b, conv_w_out, conv_b_out, peer_wq, peer_keys, peer_u, peer_v):
    B, _, D = x_prompt.shape
    depth = norm_mix.shape[0]
    assert depth == 2 and pool_w.shape[0] == 1 and conv_w_in.shape[0] == 1
    assert peer_keys.shape[1:] == (PEER_HEADS, 2, PEER_NKEYS, PEER_HALF)

    pool_w_hi, pool_w_lo = _split_param(pool_w[0])
    route = []
    for i in range(depth):
        wq_hi, wq_lo = _split_param(peer_wq[i].T)
        k1_hi, k1_lo = _split_param(peer_keys[i, :, 0])
        k2_hi, k2_lo = _split_param(peer_keys[i, :, 1])
        route.append((wq_hi, wq_lo, k1_hi, k1_lo, k2_hi, k2_lo))
    p = dict(
        norm_mix=norm_mix, norm_ffn=norm_ffn, norm_final=norm_final,
        pool_w_hi=pool_w_hi, pool_w_lo=pool_w_lo, pool_scale=pool_scale,
        conv_w_in=conv_w_in, conv_b_in=conv_b_in, conv_dw=conv_dw, conv_dw_b=conv_dw_b,
        conv_ln_g=conv_ln_g, conv_ln_b=conv_ln_b, conv_w_out=conv_w_out, conv_b_out=conv_b_out,
        route=route,
        u_bf=[peer_u[i].astype(BF16) for i in range(depth)],
        vt_bf=[peer_v[i].astype(BF16).T for i in range(depth)],
    )
    zero_pool = jnp.zeros((B, POOL_HIST, D), x_prompt.dtype)
    zero_conv = jnp.zeros((B, CONV_HIST, D), x_prompt.dtype)
    y_p, np_p, nc_p = _trunk(x_prompt, zero_pool, zero_conv, 0, p)
    past_len = 4096
    y_s, np_s, nc_s = _trunk(x_sample, cache_pool[0], state_conv[0], past_len, p)
    return (y_p, y_s, np_p, nc_p, np_s, nc_s)
```

```python
import functools
import math

import jax
import jax.numpy as jnp
from jax import lax
from jax.experimental import pallas as pl
from jax.experimental.pallas import tpu as pltpu

EPS = 1e-6
POOL_WINDOWS = (2, 4, 8, 16)
POOL_HIST = 15
POOL_PAD = 16
CONV_WIDTH = 31
CONV_HIST = 30
CONV_PAD = 32
PEER_HEADS = 8
PEER_NKEYS = 128
PEER_HALF = 64
PEER_TOPK = 16
RANK_SCALE = 1.0 / 32.0
LANES = 128
BF16_SUBLANES = 16
MXU_DIM = 256
VMEM_LIMIT = 56 * 1024 * 1024

F32 = jnp.float32
BF16 = jnp.bfloat16
NT_DIMS = (((1,), (1,)), ((), ()))


def _rmsnorm(x, g):
    ms = jnp.mean(x * x, axis=-1, keepdims=True)
    return x * lax.rsqrt(ms + EPS) * g


def _split_bf16(a):
    hi = a.astype(BF16)
    lo = (a - hi.astype(F32)).astype(BF16)
    return hi, lo


def _dot(a, b):
    return jnp.dot(a, b, preferred_element_type=F32)


def _dot3(a_hi, a_lo, b_hi, b_lo):
    return _dot(a_hi, b_hi) + _dot(a_lo, b_hi) + _dot(a_hi, b_lo)


def _dot3_nt(a_hi, a_lo, b_hi, b_lo):
    dg = functools.partial(lax.dot_general, dimension_numbers=NT_DIMS, preferred_element_type=F32)
    return dg(a_hi, b_hi) + dg(a_lo, b_hi) + dg(a_hi, b_lo)


def _pool_kernel(x_ref, hist_ref, g_ref, wh_ref, wl_ref, sc_ref, o_ref, np_ref, xp_ref, *, ts, pos0, n_s, gw):
    s = pl.program_id(1)

    @pl.when(s == 0)
    def _():
        xp_ref[0:POOL_PAD, :] = hist_ref[0]

    @pl.when(s > 0)
    def _():
        xp_ref[0:POOL_PAD, :] = xp_ref[ts:ts + POOL_PAD, :]

    x = x_ref[0]
    xn = _rmsnorm(x, g_ref[...])
    xp_ref[POOL_PAD:POOL_PAD + ts, :] = xn

    for g, win in enumerate(POOL_WINDOWS):
        c0 = g * gw
        xg = xn[:, c0:c0 + gw]
        acc = xg
        for j in range(1, win):
            acc = acc + xp_ref[POOL_PAD - j:POOL_PAD - j + ts, c0:c0 + gw]
        if pos0 >= win - 1:
            mean = acc * (1.0 / win)
        else:
            pos = pos0 + s * ts + lax.broadcasted_iota(jnp.int32, (ts, gw), 0)
            cnt = jnp.minimum(win, pos + 1).astype(F32)
            mean = acc / cnt
        d_hi, d_lo = _split_bf16(mean - xg)
        y = _dot3(d_hi, d_lo, wh_ref[g], wl_ref[g])
        o_ref[0, :, c0:c0 + gw] = x[:, c0:c0 + gw] + y * sc_ref[:, c0:c0 + gw]

    @pl.when(s == n_s - 1)
    def _():
        np_ref[0] = xp_ref[ts:ts + POOL_PAD, :]


def _pool_layer(x, hist, pos0, g, w_hi, w_lo, scale):
    B, S, D = x.shape
    ts = min(S, 512)
    n_s = S // ts
    gw = D // len(POOL_WINDOWS)
    hist_p = jnp.pad(hist, ((0, 0), (POOL_PAD - POOL_HIST, 0), (0, 0)))
    kern = functools.partial(_pool_kernel, ts=ts, pos0=pos0, n_s=n_s, gw=gw)
    out, new_hist = pl.pallas_call(
        kern,
        grid=(B, n_s),
        in_specs=[
            pl.BlockSpec((1, ts, D), lambda b, s: (b, s, 0)),
            pl.BlockSpec((1, POOL_PAD, D), lambda b, s: (b, 0, 0)),
            pl.BlockSpec((1, D), lambda b, s: (0, 0)),
            pl.BlockSpec(w_hi.shape, lambda b, s: (0, 0, 0)),
            pl.BlockSpec(w_lo.shape, lambda b, s: (0, 0, 0)),
            pl.BlockSpec((1, D), lambda b, s: (0, 0)),
        ],
        out_specs=[
            pl.BlockSpec((1, ts, D), lambda b, s: (b, s, 0)),
            pl.BlockSpec((1, POOL_PAD, D), lambda b, s: (b, 0, 0)),
        ],
        out_shape=[
            jax.ShapeDtypeStruct((B, S, D), F32),
            jax.ShapeDtypeStruct((B, POOL_PAD, D), F32),
        ],
        scratch_shapes=[pltpu.VMEM((POOL_PAD + ts, D), F32)],
        compiler_params=pltpu.CompilerParams(
            dimension_semantics=("arbitrary", "arbitrary"), vmem_limit_bytes=VMEM_LIMIT),
        name="pool_layer",
    )(x, hist_p, g.reshape(1, D), w_hi, w_lo, scale.reshape(1, D))
    return out, new_hist[:, POOL_PAD - POOL_HIST:]


def _conv_kernel(x_ref, hist_ref, g_ref, win_ref, bin_ref, dw_ref, dwb_ref, lng_ref, lnb_ref, wout_ref, bout_ref,
                 o_ref, nc_ref, up_ref, z_ref, *, ts, n_s, rb):
    s = pl.program_id(1)
    D = x_ref.shape[-1]

    @pl.when(s == 0)
    def _():
        up_ref[0:CONV_PAD, :] = hist_ref[0]

    @pl.when(s > 0)
    def _():
        up_ref[0:CONV_PAD, :] = up_ref[ts:ts + CONV_PAD, :]

    x = x_ref[0]
    xn = _rmsnorm(x, g_ref[...])
    h = _dot(xn.astype(BF16), win_ref[...]) + bin_ref[...]
    u = h[:, :D] * jax.nn.sigmoid(h[:, D:])
    up_ref[CONV_PAD:CONV_PAD + ts, :] = u

    base = CONV_PAD - CONV_HIST
    for r0 in range(0, ts, rb):
        for c0 in range(0, D, LANES):
            acc = jnp.zeros((rb, LANES), F32) + dwb_ref[:, c0:c0 + LANES]
            for k in range(CONV_WIDTH):
                acc = acc + up_ref[base + r0 + k:base + r0 + k + rb, c0:c0 + LANES] * dw_ref[k:k + 1, c0:c0 + LANES]
            z_ref[r0:r0 + rb, c0:c0 + LANES] = acc

    z = z_ref[...]
    mu = jnp.mean(z, axis=-1, keepdims=True)
    zc = z - mu
    var = jnp.mean(zc * zc, axis=-1, keepdims=True)
    zn = zc * lax.rsqrt(var + EPS) * lng_ref[...] + lnb_ref[...]
    act = zn * jax.nn.sigmoid(zn)
    y = _dot(act.astype(BF16), wout_ref[...]) + bout_ref[...]
    o_ref[0] = x + y

    @pl.when(s == n_s - 1)
    def _():
        nc_ref[0] = up_ref[ts:ts + CONV_PAD, :]


def _conv_layer(x, hist, g, w_in, b_in, dw, dw_b, ln_g, ln_b, w_out, b_out):
    B, S, D = x.shape
    ts = min(S, 256)
    n_s = S // ts
    hist_p = jnp.pad(hist, ((0, 0), (CONV_PAD - CONV_HIST, 0), (0, 0)))
    kern = functools.partial(_conv_kernel, ts=ts, n_s=n_s, rb=min(ts, 64))
    row = lambda a: a.reshape(1, -1)
    const2 = lambda b, s: (0, 0)
    out, new_hist = pl.pallas_call(
        kern,
        grid=(B, n_s),
        in_specs=[
            pl.BlockSpec((1, ts, D), lambda b, s: (b, s, 0)),
            pl.BlockSpec((1, CONV_PAD, D), lambda b, s: (b, 0, 0)),
            pl.BlockSpec((1, D), const2),
            pl.BlockSpec((D, 2 * D), const2),
            pl.BlockSpec((1, 2 * D), const2),
            pl.BlockSpec((CONV_WIDTH, D), const2),
            pl.BlockSpec((1, D), const2),
            pl.BlockSpec((1, D), const2),
            pl.BlockSpec((1, D), const2),
            pl.BlockSpec((D, D), const2),
            pl.BlockSpec((1, D), const2),
        ],
        out_specs=[
            pl.BlockSpec((1, ts, D), lambda b, s: (b, s, 0)),
            pl.BlockSpec((1, CONV_PAD, D), lambda b, s: (b, 0, 0)),
        ],
        out_shape=[
            jax.ShapeDtypeStruct((B, S, D), F32),
            jax.ShapeDtypeStruct((B, CONV_PAD, D), F32),
        ],
        scratch_shapes=[pltpu.VMEM((CONV_PAD + ts, D), F32), pltpu.VMEM((ts, D), F32)],
        compiler_params=pltpu.CompilerParams(
            dimension_semantics=("arbitrary", "arbitrary"), vmem_limit_bytes=VMEM_LIMIT),
        name="conv_layer",
    )(x, hist_p, row(g), w_in.astype(BF16), row(b_in), dw, row(dw_b), row(ln_g), row(ln_b),
      w_out.astype(BF16), row(b_out))
    return out, new_hist[:, CONV_PAD - CONV_HIST:]


def _top16(s, v_ref):
    work = s
    rank = jnp.full(s.shape, float(PEER_TOPK), F32)
    for r in range(PEER_TOPK):
        m = jnp.max(work, axis=0, keepdims=True)
        v_ref[r:r + 1, :] = m
        hit = work == m
        rank = jnp.where(hit, float(r), rank)
        work = jnp.where(hit, -jnp.inf, work)
    return rank


def _pair_select(v1, v2):
    row = lax.broadcasted_iota(jnp.int32, (8, LANES), 0)
    neg = -jnp.inf
    cands = [v1[0:1] + v2[0:8], v1[0:1] + v2[8:16], v1[1:2] + v2[0:8]]
    for a in range(2, 8):
        cands.append(jnp.where(row < PEER_TOPK // (a + 1), v1[a:a + 1] + v2[0:8], neg))
    cands.append(v1[8:16] + v2[0:1])

    work = list(cands)
    vals = []
    for _ in range(PEER_TOPK):
        m8 = work[0]
        for w in work[1:]:
            m8 = jnp.maximum(m8, w)
        m = jnp.max(m8, axis=0, keepdims=True)
        vals.append(m)
        work = [jnp.where(w == m, neg, w) for w in work]
    tau, top = vals[-1], vals[0]
    z = jnp.ones_like(top)
    for v in vals[1:]:
        z = z + jnp.exp(v - top)

    sel = [(c >= tau).astype(F32) for c in cands]
    colsum = lambda a: jnp.sum(a, axis=0, keepdims=True)
    n_lo = [colsum(sel[0]) + colsum(sel[1])] + [colsum(sel[i]) for i in range(2, 9)]
    n_hi = sel[9]
    return n_lo, n_hi, z


def _route_kernel(x_ref, g_ref, wqh_ref, wql_ref, k1h_ref, k1l_ref, k2h_ref, k2l_ref,
                  xn_ref, r2_ref, e2_ref, n1_ref, e1_ref, qt_ref, s_ref, v1_ref, v2_ref, *, tt):
    xn = _rmsnorm(x_ref[...], g_ref[...])
    x_hi, x_lo = _split_bf16(xn)
    xn_ref[...] = x_hi
    qt_ref[...] = _dot3_nt(wqh_ref[...], wql_ref[...], x_hi, x_lo)

    def head_body(h, carry):
        q0 = pl.multiple_of(h * (2 * PEER_HALF), 2 * PEER_HALF)
        q1_hi, q1_lo = _split_bf16(qt_ref[pl.ds(q0, PEER_HALF), :])
        q2_hi, q2_lo = _split_bf16(qt_ref[pl.ds(q0 + PEER_HALF, PEER_HALF), :])
        s_ref[0] = _dot3(k1h_ref[h], k1l_ref[h], q1_hi, q1_lo)
        s_ref[1] = _dot3(k2h_ref[h], k2l_ref[h], q2_hi, q2_lo)
        for c0 in range(0, tt, LANES):
            cols = slice(c0, c0 + LANES)
            rank1 = _top16(s_ref[0, :, cols], v1_ref)
            rank2 = _top16(s_ref[1, :, cols], v2_ref)
            r2_ref[h, :, cols] = (rank2 * RANK_SCALE).astype(BF16)
            v1 = v1_ref[...]
            v2 = v2_ref[...]
            e2_ref[h, :, cols] = jnp.exp(s_ref[1, :, cols] - v2[0:1]).astype(BF16)
            n_lo, n_hi, z = _pair_select(v1, v2)
            n1 = jnp.zeros((PEER_NKEYS, LANES), F32)
            for a in range(PEER_TOPK):
                n_a = n_lo[a] if a < 8 else n_hi[a - 8:a - 7]
                n1 = jnp.where(rank1 == float(a), n_a, n1)
            n1_ref[h, :, cols] = n1 * RANK_SCALE + (1.0 - RANK_SCALE)
            e1_ref[h, :, cols] = jnp.exp(s_ref[0, :, cols] - v1[0:1]) * (1.0 / z)
        return carry

    lax.fori_loop(0, PEER_HEADS, head_body, 0)


def _peer_route(x, g, wq_hi, wq_lo, k1_hi, k1_lo, k2_hi, k2_lo):
    T, D = x.shape
    tt = min(T, 512)
    hd = wq_hi.shape[0]
    kern = functools.partial(_route_kernel, tt=tt)
    c2 = lambda i: (0, 0)
    c3 = lambda i: (0, 0, 0)
    tab16 = jax.ShapeDtypeStruct((PEER_HEADS, PEER_NKEYS, T), BF16)
    tabf = jax.ShapeDtypeStruct((PEER_HEADS, PEER_NKEYS, T), F32)
    tab_spec = pl.BlockSpec((PEER_HEADS, PEER_NKEYS, tt), lambda i: (0, 0, i))
    return pl.pallas_call(
        kern,
        grid=(T // tt,),
        in_specs=[
            pl.BlockSpec((tt, D), lambda i: (i, 0)),
            pl.BlockSpec((1, D), c2),
            pl.BlockSpec((hd, D), c2),
            pl.BlockSpec((hd, D), c2),
            pl.BlockSpec(k1_hi.shape, c3),
            pl.BlockSpec(k1_hi.shape, c3),
            pl.BlockSpec(k1_hi.shape, c3),
            pl.BlockSpec(k1_hi.shape, c3),
        ],
        out_specs=[pl.BlockSpec((tt, D), lambda i: (i, 0)), tab_spec, tab_spec, tab_spec, tab_spec],
        out_shape=[jax.ShapeDtypeStruct((T, D), BF16), tab16, tab16, tabf, tabf],
        scratch_shapes=[
            pltpu.VMEM((hd, tt), F32),
            pltpu.VMEM((2, PEER_NKEYS, tt), F32),
            pltpu.VMEM((PEER_TOPK, LANES), F32),
            pltpu.VMEM((PEER_TOPK, LANES), F32),
        ],
        compiler_params=pltpu.CompilerParams(
            dimension_semantics=("arbitrary",), vmem_limit_bytes=VMEM_LIMIT),
        name="peer_route",
    )(x, g.reshape(1, D), wq_hi, wq_lo, k1_hi, k1_lo, k2_hi, k2_lo)


def _gelu(h):
    return 0.5 * h * (1.0 + lax.erf(h * (1.0 / math.sqrt(2.0))))


def _row_bf16(ref, h, k, cols):
    return jnp.broadcast_to(ref[h, 0, k:k + 1, cols], (BF16_SUBLANES, LANES)).astype(BF16)


def _expert_step(xn_ref, u_ref, vt_ref, r2_ref, e2_ref, n1_ref, e1_ref, acc_ref, w_ref, hid_w, hid_r, *, tt, ec):
    hid_w[...] = lax.dot_general(u_ref[...], xn_ref[...], NT_DIMS, preferred_element_type=F32)

    part = None
    for kt in range(ec // MXU_DIM):
        for k in range(kt * (MXU_DIM // PEER_NKEYS), (kt + 1) * (MXU_DIM // PEER_NKEYS)):
            for c0 in range(0, tt, LANES):
                cols = slice(c0, c0 + LANES)
                n1 = [_row_bf16(n1_ref, h, k, cols) for h in range(PEER_HEADS)]
                e1 = [_row_bf16(e1_ref, h, k, cols) for h in range(PEER_HEADS)]
                for i2 in range(0, PEER_NKEYS, BF16_SUBLANES):
                    sub = slice(i2, i2 + BF16_SUBLANES)
                    rows = slice(k * PEER_NKEYS + i2, k * PEER_NKEYS + i2 + BF16_SUBLANES)
                    gate = None
                    for h in range(PEER_HEADS):
                        sel = jnp.floor(n1[h] - r2_ref[h, sub, cols])
                        g_h = (e1[h] * sel) * e2_ref[h, sub, cols]
                        gate = g_h if gate is None else gate + g_h
                    w_ref[rows, cols] = gate * _gelu(hid_r[rows, cols]).astype(BF16)
        kk = slice(kt * MXU_DIM, (kt + 1) * MXU_DIM)
        p = _dot(vt_ref[:, kk], w_ref[kk, :])
        part = p if part is None else part + p
    acc_ref[...] += part


def _expert_kernel(xn_ref, x_ref, u_ref, vt_ref, r2_ref, e2_ref, n1_ref, e1_ref, gf_ref,
                   o_ref, acc_ref, hid_a, hid_b, w_ref, *, tt, ec, n_j, final):
    j = pl.program_id(1)
    step = functools.partial(_expert_step, xn_ref, u_ref, vt_ref, r2_ref, e2_ref, n1_ref, e1_ref, acc_ref, w_ref,
                             tt=tt, ec=ec)

    @pl.when(j == 0)
    def _():
        acc_ref[...] = jnp.zeros_like(acc_ref)
        hid_b[...] = jnp.zeros_like(hid_b)

    @pl.when(j % 2 == 0)
    def _():
        step(hid_a, hid_b)

    @pl.when(j % 2 == 1)
    def _():
        step(hid_b, hid_a)

    @pl.when(j == n_j)
    def _():
        out = x_ref[...] + acc_ref[...].T
        if final:
            out = _rmsnorm(out, gf_ref[...])
        o_ref[...] = out


def _peer_experts(x, xn, u_bf, vt_bf, r2, e2, n1, e1, g_final, final):
    T, D = x.shape
    n_exp = u_bf.shape[0]
    tt = min(T, 512)
    ec = 1024
    n_j = n_exp // ec
    assert n_j % 2 == 0 and ec % MXU_DIM == 0
    i1c = ec // PEER_NKEYS
    n1 = n1.reshape(PEER_HEADS, PEER_NKEYS // i1c, i1c, T)
    e1 = e1.reshape(PEER_HEADS, PEER_NKEYS // i1c, i1c, T)
    kern = functools.partial(_expert_kernel, tt=tt, ec=ec, n_j=n_j, final=final)
    cur = lambda j: jnp.minimum(j, n_j - 1)
    prev = lambda j: jnp.maximum(j - 1, 0)
    tab_spec = pl.BlockSpec((PEER_HEADS, PEER_NKEYS, tt), lambda i, j: (0, 0, i))
    row_spec = pl.BlockSpec((PEER_HEADS, 1, i1c, tt), lambda i, j: (0, prev(j), 0, i))
    return pl.pallas_call(
        kern,
        grid=(T // tt, n_j + 1),
        in_specs=[
            pl.BlockSpec((tt, D), lambda i, j: (i, 0)),
            pl.BlockSpec((tt, D), lambda i, j: (i, 0)),
            pl.BlockSpec((ec, D), lambda i, j: (cur(j), 0)),
            pl.BlockSpec((D, ec), lambda i, j: (0, prev(j))),
            tab_spec, tab_spec, row_spec, row_spec,
            pl.BlockSpec((1, D), lambda i, j: (0, 0)),
        ],
        out_specs=pl.BlockSpec((tt, D), lambda i, j: (i, 0)),
        out_shape=jax.ShapeDtypeStruct((T, D), F32),
        scratch_shapes=[
            pltpu.VMEM((D, tt), F32),
            pltpu.VMEM((ec, tt), F32),
            pltpu.VMEM((ec, tt), F32),
            pltpu.VMEM((ec, tt), BF16),
        ],
        compiler_params=pltpu.CompilerParams(
            dimension_semantics=("arbitrary", "arbitrary"), vmem_limit_bytes=VMEM_LIMIT),
        name="peer_experts",
    )(xn, x, u_bf, vt_bf, r2, e2, n1, e1, g_final.reshape(1, D))


def _split_param(a):
    hi = a.astype(BF16)
    return hi, (a - hi.astype(F32)).astype(BF16)


def _trunk(x, pool_hist, conv_hist, pos0, p):
    B, S, D = x.shape
    x, new_pool = _pool_layer(x, pool_hist, pos0, p["norm_mix"][0], p["pool_w_hi"], p["pool_w_lo"], p["pool_scale"][0])
    for i in range(2):
        if i == 1:
            x, new_conv = _conv_layer(x, conv_hist, p["norm_mix"][1], p["conv_w_in"][0], p["conv_b_in"][0],
                                      p["conv_dw"][0], p["conv_dw_b"][0], p["conv_ln_g"][0], p["conv_ln_b"][0],
                                      p["conv_w_out"][0], p["conv_b_out"][0])
        xt = x.reshape(B * S, D)
        xn, r2, e2, n1, e1 = _peer_route(xt, p["norm_ffn"][i], *p["route"][i])
        xt = _peer_experts(xt, xn, p["u_bf"][i], p["vt_bf"][i], r2, e2, n1, e1, p["norm_final"], final=(i == 1))
        x = xt.reshape(B, S, D)
    return x, new_pool[None], new_conv[None]


def kernel(x_prompt, x_sample, cache_pool, state_conv, norm_mix, norm_ffn, norm_final, pool_w, pool_scale, conv_w_in, conv_b_in, conv_dw, conv_dw_b, conv_ln_g, conv_ln_b, conv_w_out, conv_b_out, peer_wq, peer_keys, peer_u, peer_v):
    B, _, D = x_prompt.shape
    depth = norm_mix.shape[0]
    assert depth == 2 and pool_w.shape[0] == 1 and conv_w_in.shape[0] == 1
    assert peer_keys.shape[1:] == (PEER_HEADS, 2, PEER_NKEYS, PEER_HALF)

    pool_w_hi, pool_w_lo = _split_param(pool_w[0])
    route = []
    for i in range(depth):
        wq_hi, wq_lo = _split_param(peer_wq[i].T)
        k1_hi, k1_lo = _split_param(peer_keys[i, :, 0])
        k2_hi, k2_lo = _split_param(peer_keys[i, :, 1])
        route.append((wq_hi, wq_lo, k1_hi, k1_lo, k2_hi, k2_lo))
    p = dict(
        norm_mix=norm_mix, norm_ffn=norm_ffn, norm_final=norm_final,
        pool_w_hi=pool_w_hi, pool_w_lo=pool_w_lo, pool_scale=pool_scale,
        conv_w_in=conv_w_in, conv_b_in=conv_b_in, conv_dw=conv_dw, conv_dw_b=conv_dw_b,
        conv_ln_g=conv_ln_g, conv_ln_b=conv_ln_b, conv_w_out=conv_w_out, conv_b_out=conv_b_out,
        route=route,
        u_bf=[peer_u[i].astype(BF16) for i in range(depth)],
        vt_bf=[peer_v[i].astype(BF16).T for i in range(depth)],
    )
    zero_pool = jnp.zeros((B, POOL_HIST, D), x_prompt.dtype)
    zero_conv = jnp.zeros((B, CONV_HIST, D), x_prompt.dtype)
    y_p, np_p, nc_p = _trunk(x_prompt, zero_pool, zero_conv, 0, p)
    past_len = 4096
    y_s, np_s, nc_s = _trunk(x_sample, cache_pool[0], state_conv[0], past_len, p)
    return (y_p, y_s, np_p, nc_p, np_s, nc_s)
```

```python
import functools
import math

import jax
import jax.numpy as jnp
from jax import lax
from jax.experimental import pallas as pl
from jax.experimental.pallas import tpu as pltpu

EPS = 1e-6
POOL_WINDOWS = (2, 4, 8, 16)
POOL_HIST = 15
POOL_PAD = 16
CONV_WIDTH = 31
CONV_HIST = 30
CONV_PAD = 32
PEER_HEADS = 8
PEER_NKEYS = 128
PEER_HALF = 64
PEER_TOPK = 16
RANK_SCALE = 1.0 / 32.0
LANES = 128
BF16_SUBLANES = 16
MXU_DIM = 256
EXPERT_CHUNK = 1024
VMEM_LIMIT = 56 * 1024 * 1024

F32 = jnp.float32
BF16 = jnp.bfloat16
NT_DIMS = (((1,), (1,)), ((), ()))


def _rmsnorm(x, g):
    ms = jnp.mean(x * x, axis=-1, keepdims=True)
    return x * lax.rsqrt(ms + EPS) * g


def _split_bf16(a):
    hi = a.astype(BF16)
    lo = (a - hi.astype(F32)).astype(BF16)
    return hi, lo


def _dot(a, b):
    return jnp.dot(a, b, preferred_element_type=F32)


def _dot3(a_hi, a_lo, b_hi, b_lo):
    return _dot(a_hi, b_hi) + _dot(a_lo, b_hi) + _dot(a_hi, b_lo)


def _dot3_nt(a_hi, a_lo, b_hi, b_lo):
    dg = functools.partial(lax.dot_general, dimension_numbers=NT_DIMS, preferred_element_type=F32)
    return dg(a_hi, b_hi) + dg(a_lo, b_hi) + dg(a_hi, b_lo)


def _pool_kernel(x_ref, hist_ref, g_ref, wh_ref, wl_ref, sc_ref, o_ref, np_ref, xp_ref, *, ts, pos0, n_s, gw):
    s = pl.program_id(1)

    @pl.when(s == 0)
    def _():
        xp_ref[0:POOL_PAD, :] = hist_ref[0]

    @pl.when(s > 0)
    def _():
        xp_ref[0:POOL_PAD, :] = xp_ref[ts:ts + POOL_PAD, :]

    x = x_ref[0]
    xn = _rmsnorm(x, g_ref[...])
    xp_ref[POOL_PAD:POOL_PAD + ts, :] = xn

    for g, win in enumerate(POOL_WINDOWS):
        c0 = g * gw
        xg = xn[:, c0:c0 + gw]
        acc = xg
        for j in range(1, win):
            acc = acc + xp_ref[POOL_PAD - j:POOL_PAD - j + ts, c0:c0 + gw]
        if pos0 >= win - 1:
            mean = acc * (1.0 / win)
        else:
            pos = pos0 + s * ts + lax.broadcasted_iota(jnp.int32, (ts, gw), 0)
            cnt = jnp.minimum(win, pos + 1).astype(F32)
            mean = acc / cnt
        d_hi, d_lo = _split_bf16(mean - xg)
        y = _dot3(d_hi, d_lo, wh_ref[g], wl_ref[g])
        o_ref[0, :, c0:c0 + gw] = x[:, c0:c0 + gw] + y * sc_ref[:, c0:c0 + gw]

    @pl.when(s == n_s - 1)
    def _():
        np_ref[0] = xp_ref[ts:ts + POOL_PAD, :]


def _pool_layer(x, hist, pos0, g, w_hi, w_lo, scale):
    B, S, D = x.shape
    ts = min(S, 512)
    n_s = S // ts
    gw = D // len(POOL_WINDOWS)
    hist_p = jnp.pad(hist, ((0, 0), (POOL_PAD - POOL_HIST, 0), (0, 0)))
    kern = functools.partial(_pool_kernel, ts=ts, pos0=pos0, n_s=n_s, gw=gw)
    out, new_hist = pl.pallas_call(
        kern,
        grid=(B, n_s),
        in_specs=[
            pl.BlockSpec((1, ts, D), lambda b, s: (b, s, 0)),
            pl.BlockSpec((1, POOL_PAD, D), lambda b, s: (b, 0, 0)),
            pl.BlockSpec((1, D), lambda b, s: (0, 0)),
            pl.BlockSpec(w_hi.shape, lambda b, s: (0, 0, 0)),
            pl.BlockSpec(w_lo.shape, lambda b, s: (0, 0, 0)),
            pl.BlockSpec((1, D), lambda b, s: (0, 0)),
        ],
        out_specs=[
            pl.BlockSpec((1, ts, D), lambda b, s: (b, s, 0)),
            pl.BlockSpec((1, POOL_PAD, D), lambda b, s: (b, 0, 0)),
        ],
        out_shape=[
            jax.ShapeDtypeStruct((B, S, D), F32),
            jax.ShapeDtypeStruct((B, POOL_PAD, D), F32),
        ],
        scratch_shapes=[pltpu.VMEM((POOL_PAD + ts, D), F32)],
        compiler_params=pltpu.CompilerParams(
            dimension_semantics=("arbitrary", "arbitrary"), vmem_limit_bytes=VMEM_LIMIT),
        name="pool_layer",
    )(x, hist_p, g.reshape(1, D), w_hi, w_lo, scale.reshape(1, D))
    return out, new_hist[:, POOL_PAD - POOL_HIST:]


def _conv_kernel(x_ref, hist_ref, g_ref, win_ref, bin_ref, dw_ref, dwb_ref, lng_ref, lnb_ref, wout_ref, bout_ref,
                 o_ref, nc_ref, up_ref, z_ref, *, ts, n_s, rb):
    s = pl.program_id(1)
    D = x_ref.shape[-1]

    @pl.when(s == 0)
    def _():
        up_ref[0:CONV_PAD, :] = hist_ref[0]

    @pl.when(s > 0)
    def _():
        up_ref[0:CONV_PAD, :] = up_ref[ts:ts + CONV_PAD, :]

    x = x_ref[0]
    xn = _rmsnorm(x, g_ref[...])
    h = _dot(xn.astype(BF16), win_ref[...]) + bin_ref[...]
    u = h[:, :D] * jax.nn.sigmoid(h[:, D:])
    up_ref[CONV_PAD:CONV_PAD + ts, :] = u

    base = CONV_PAD - CONV_HIST
    for r0 in range(0, ts, rb):
        for c0 in range(0, D, LANES):
            acc = jnp.zeros((rb, LANES), F32) + dwb_ref[:, c0:c0 + LANES]
            for k in range(CONV_WIDTH):
                acc = acc + up_ref[base + r0 + k:base + r0 + k + rb, c0:c0 + LANES] * dw_ref[k:k + 1, c0:c0 + LANES]
            z_ref[r0:r0 + rb, c0:c0 + LANES] = acc

    z = z_ref[...]
    mu = jnp.mean(z, axis=-1, keepdims=True)
    zc = z - mu
    var = jnp.mean(zc * zc, axis=-1, keepdims=True)
    zn = zc * lax.rsqrt(var + EPS) * lng_ref[...] + lnb_ref[...]
    act = zn * jax.nn.sigmoid(zn)
    y = _dot(act.astype(BF16), wout_ref[...]) + bout_ref[...]
    o_ref[0] = x + y

    @pl.when(s == n_s - 1)
    def _():
        nc_ref[0] = up_ref[ts:ts + CONV_PAD, :]


def _conv_layer(x, hist, g, w_in, b_in, dw, dw_b, ln_g, ln_b, w_out, b_out):
    B, S, D = x.shape
    ts = min(S, 256)
    n_s = S // ts
    hist_p = jnp.pad(hist, ((0, 0), (CONV_PAD - CONV_HIST, 0), (0, 0)))
    kern = functools.partial(_conv_kernel, ts=ts, n_s=n_s, rb=min(ts, 64))
    row = lambda a: a.reshape(1, -1)
    const2 = lambda b, s: (0, 0)
    out, new_hist = pl.pallas_call(
        kern,
        grid=(B, n_s),
        in_specs=[
            pl.BlockSpec((1, ts, D), lambda b, s: (b, s, 0)),
            pl.BlockSpec((1, CONV_PAD, D), lambda b, s: (b, 0, 0)),
            pl.BlockSpec((1, D), const2),
            pl.BlockSpec((D, 2 * D), const2),
            pl.BlockSpec((1, 2 * D), const2),
            pl.BlockSpec((CONV_WIDTH, D), const2),
            pl.BlockSpec((1, D), const2),
            pl.BlockSpec((1, D), const2),
            pl.BlockSpec((1, D), const2),
            pl.BlockSpec((D, D), const2),
            pl.BlockSpec((1, D), const2),
        ],
        out_specs=[
            pl.BlockSpec((1, ts, D), lambda b, s: (b, s, 0)),
            pl.BlockSpec((1, CONV_PAD, D), lambda b, s: (b, 0, 0)),
        ],
        out_shape=[
            jax.ShapeDtypeStruct((B, S, D), F32),
            jax.ShapeDtypeStruct((B, CONV_PAD, D), F32),
        ],
        scratch_shapes=[pltpu.VMEM((CONV_PAD + ts, D), F32), pltpu.VMEM((ts, D), F32)],
        compiler_params=pltpu.CompilerParams(
            dimension_semantics=("arbitrary", "arbitrary"), vmem_limit_bytes=VMEM_LIMIT),
        name="conv_layer",
    )(x, hist_p, row(g), w_in.astype(BF16), row(b_in), dw, row(dw_b), row(ln_g), row(ln_b),
      w_out.astype(BF16), row(b_out))
    return out, new_hist[:, CONV_PAD - CONV_HIST:]


def _top16(s, v_ref):
    work = s
    rank = jnp.full(s.shape, float(PEER_TOPK), F32)
    for r in range(PEER_TOPK):
        m = jnp.max(work, axis=0, keepdims=True)
        v_ref[r:r + 1, :] = m
        hit = work == m
        rank = jnp.where(hit, float(r), rank)
        work = jnp.where(hit, -jnp.inf, work)
    return rank


def _pair_select(v1, v2):
    row = lax.broadcasted_iota(jnp.int32, (8, LANES), 0)
    neg = -jnp.inf
    cands = [v1[0:1] + v2[0:8], v1[0:1] + v2[8:16], v1[1:2] + v2[0:8]]
    for a in range(2, 8):
        cands.append(jnp.where(row < PEER_TOPK // (a + 1), v1[a:a + 1] + v2[0:8], neg))
    cands.append(v1[8:16] + v2[0:1])

    work = list(cands)
    vals = []
    for _ in range(PEER_TOPK):
        m8 = work[0]
        for w in work[1:]:
            m8 = jnp.maximum(m8, w)
        m = jnp.max(m8, axis=0, keepdims=True)
        vals.append(m)
        work = [jnp.where(w == m, neg, w) for w in work]
    tau, top = vals[-1], vals[0]
    z = jnp.ones_like(top)
    for v in vals[1:]:
        z = z + jnp.exp(v - top)

    sel = [(c >= tau).astype(F32) for c in cands]
    colsum = lambda a: jnp.sum(a, axis=0, keepdims=True)
    n_lo = [colsum(sel[0]) + colsum(sel[1])] + [colsum(sel[i]) for i in range(2, 9)]
    n_hi = sel[9]
    return n_lo, n_hi, z


def _route_kernel(x_ref, g_ref, wqh_ref, wql_ref, k1h_ref, k1l_ref, k2h_ref, k2l_ref,
                  xn_ref, re_ref, n1_ref, e1_ref, qt_ref, s_ref, v1_ref, v2_ref, *, tt):
    xn = _rmsnorm(x_ref[...], g_ref[...])
    x_hi, x_lo = _split_bf16(xn)
    xn_ref[...] = x_hi
    qt_ref[...] = _dot3_nt(wqh_ref[...], wql_ref[...], x_hi, x_lo)

    def head_body(h, carry):
        q0 = pl.multiple_of(h * (2 * PEER_HALF), 2 * PEER_HALF)
        q1_hi, q1_lo = _split_bf16(qt_ref[pl.ds(q0, PEER_HALF), :])
        q2_hi, q2_lo = _split_bf16(qt_ref[pl.ds(q0 + PEER_HALF, PEER_HALF), :])
        s1 = _dot3(k1h_ref[h], k1l_ref[h], q1_hi, q1_lo)
        s2 = _dot3(k2h_ref[h], k2l_ref[h], q2_hi, q2_lo)
        for c in range(tt // LANES):
            s_ref[0, c] = s1[:, c * LANES:(c + 1) * LANES]
            s_ref[1, c] = s2[:, c * LANES:(c + 1) * LANES]
        for c in range(tt // LANES):
            cols = slice(c * LANES, (c + 1) * LANES)
            rank1 = _top16(s_ref[0, c], v1_ref)
            rank2 = _top16(s_ref[1, c], v2_ref)
            v1 = v1_ref[...]
            v2 = v2_ref[...]
            r2 = (rank2 * RANK_SCALE).astype(BF16)
            e2 = jnp.exp(s_ref[1, c] - v2[0:1]).astype(BF16)
            for sub in range(PEER_NKEYS // BF16_SUBLANES):
                rows = slice(sub * BF16_SUBLANES, (sub + 1) * BF16_SUBLANES)
                re_ref[c, sub, h, 0] = r2[rows]
                re_ref[c, sub, h, 1] = e2[rows]
            n_lo, n_hi, z = _pair_select(v1, v2)
            n1 = jnp.zeros((PEER_NKEYS, LANES), F32)
            for a in range(PEER_TOPK):
                n_a = n_lo[a] if a < 8 else n_hi[a - 8:a - 7]
                n1 = jnp.where(rank1 == float(a), n_a, n1)
            n1_ref[h, :, cols] = n1 * RANK_SCALE + (1.0 - RANK_SCALE)
            e1_ref[h, :, cols] = jnp.exp(s_ref[0, c] - v1[0:1]) * (1.0 / z)
        return carry

    lax.fori_loop(0, PEER_HEADS, head_body, 0)


def _re_shape(n_tokens):
    return (n_tokens // LANES, PEER_NKEYS // BF16_SUBLANES, PEER_HEADS, 2, BF16_SUBLANES, LANES)


def _peer_route(x, g, wq_hi, wq_lo, k1_hi, k1_lo, k2_hi, k2_lo):
    T, D = x.shape
    tt = min(T, 512)
    hd = wq_hi.shape[0]
    kern = functools.partial(_route_kernel, tt=tt)
    c2 = lambda i: (0, 0)
    c3 = lambda i: (0, 0, 0)
    tabf = jax.ShapeDtypeStruct((PEER_HEADS, PEER_NKEYS, T), F32)
    tab_spec = pl.BlockSpec((PEER_HEADS, PEER_NKEYS, tt), lambda i: (0, 0, i))
    re_shape = _re_shape(T)
    re_spec = pl.BlockSpec(_re_shape(tt), lambda i: (i, 0, 0, 0, 0, 0))
    return pl.pallas_call(
        kern,
        grid=(T // tt,),
        in_specs=[
            pl.BlockSpec((tt, D), lambda i: (i, 0)),
            pl.BlockSpec((1, D), c2),
            pl.BlockSpec((hd, D), c2),
            pl.BlockSpec((hd, D), c2),
            pl.BlockSpec(k1_hi.shape, c3),
            pl.BlockSpec(k1_hi.shape, c3),
            pl.BlockSpec(k1_hi.shape, c3),
            pl.BlockSpec(k1_hi.shape, c3),
        ],
        out_specs=[pl.BlockSpec((tt, D), lambda i: (i, 0)), re_spec, tab_spec, tab_spec],
        out_shape=[jax.ShapeDtypeStruct((T, D), BF16), jax.ShapeDtypeStruct(re_shape, BF16), tabf, tabf],
        scratch_shapes=[
            pltpu.VMEM((hd, tt), F32),
            pltpu.VMEM((2, tt // LANES, PEER_NKEYS, LANES), F32),
            pltpu.VMEM((PEER_TOPK, LANES), F32),
            pltpu.VMEM((PEER_TOPK, LANES), F32),
        ],
        compiler_params=pltpu.CompilerParams(
            dimension_semantics=("arbitrary",), vmem_limit_bytes=VMEM_LIMIT),
        name="peer_route",
    )(x, g.reshape(1, D), wq_hi, wq_lo, k1_hi, k1_lo, k2_hi, k2_lo)


def _gelu(h):
    return 0.5 * h * (1.0 + lax.erf(h * (1.0 / math.sqrt(2.0))))


def _row_bf16(ref, h, k, cols):
    return jnp.broadcast_to(ref[h, 0, k:k + 1, cols], (BF16_SUBLANES, LANES)).astype(BF16)


def _expert_step(xn_ref, u_ref, vt_ref, re_ref, n1_ref, e1_ref, acc_ref, w_ref, hid_w, hid_r, *, tt, ec):
    tw = min(tt, MXU_DIM)
    kpt = MXU_DIM // PEER_NKEYS

    def token_tile(n, carry):
        t0 = pl.multiple_of(n * tw, tw)
        tok = pl.ds(t0, tw)
        hid_w[:, tok] = lax.dot_general(u_ref[...], xn_ref[tok, :], NT_DIMS, preferred_element_type=F32)

        part = None
        for kt in range(ec // MXU_DIM):
            ks = range(kt * kpt, (kt + 1) * kpt)
            for cc in range(tw // LANES):
                c = n * (tw // LANES) + cc
                cols = pl.ds(pl.multiple_of(t0 + cc * LANES, LANES), LANES)
                wcols = slice(cc * LANES, (cc + 1) * LANES)
                n1 = [[_row_bf16(n1_ref, h, k, cols) for h in range(PEER_HEADS)] for k in ks]
                e1 = [[_row_bf16(e1_ref, h, k, cols) for h in range(PEER_HEADS)] for k in ks]
                for sub in range(PEER_NKEYS // BF16_SUBLANES):
                    gates = [None] * kpt
                    for h in range(PEER_HEADS):
                        r2 = re_ref[c, sub, h, 0]
                        e2 = re_ref[c, sub, h, 1]
                        for q in range(kpt):
                            g_h = (e1[q][h] * jnp.floor(n1[q][h] - r2)) * e2
                            gates[q] = g_h if gates[q] is None else gates[q] + g_h
                    for q, k in enumerate(ks):
                        r0 = k * PEER_NKEYS + sub * BF16_SUBLANES
                        rows = slice(r0, r0 + BF16_SUBLANES)
                        w_ref[rows, wcols] = gates[q] * _gelu(hid_r[rows, cols]).astype(BF16)
            kk = slice(kt * MXU_DIM, (kt + 1) * MXU_DIM)
            p = _dot(vt_ref[0, :, kk], w_ref[kk, :])
            part = p if part is None else part + p
        acc_ref[:, tok] += part
        return carry

    lax.fori_loop(0, tt // tw, token_tile, 0)


def _expert_kernel(xn_ref, x_ref, u_ref, vt_ref, re_ref, n1_ref, e1_ref, gf_ref,
                   o_ref, acc_ref, hid_a, hid_b, w_ref, *, tt, ec, n_j, final):
    j = pl.program_id(1)
    step = functools.partial(_expert_step, xn_ref, u_ref, vt_ref, re_ref, n1_ref, e1_ref, acc_ref, w_ref,
                             tt=tt, ec=ec)

    @pl.when(j == 0)
    def _():
        acc_ref[...] = jnp.zeros_like(acc_ref)
        hid_b[...] = jnp.zeros_like(hid_b)

    @pl.when(j % 2 == 0)
    def _():
        step(hid_a, hid_b)

    @pl.when(j % 2 == 1)
    def _():
        step(hid_b, hid_a)

    @pl.when(j == n_j)
    def _():
        out = x_ref[...] + acc_ref[...].T
        if final:
            out = _rmsnorm(out, gf_ref[...])
        o_ref[...] = out


def _peer_experts(x, xn, u_bf, vt_bf, re, n1, e1, g_final, final):
    T, D = x.shape
    n_exp = u_bf.shape[0]
    tt = min(T, 512)
    ec = EXPERT_CHUNK
    n_j = n_exp // ec
    assert n_j % 2 == 0 and ec % MXU_DIM == 0
    i1c = ec // PEER_NKEYS
    n1 = n1.reshape(PEER_HEADS, PEER_NKEYS // i1c, i1c, T)
    e1 = e1.reshape(PEER_HEADS, PEER_NKEYS // i1c, i1c, T)
    kern = functools.partial(_expert_kernel, tt=tt, ec=ec, n_j=n_j, final=final)
    cur = lambda j: jnp.minimum(j, n_j - 1)
    prev = lambda j: jnp.maximum(j - 1, 0)
    re_spec = pl.BlockSpec(_re_shape(tt), lambda i, j: (i, 0, 0, 0, 0, 0))
    row_spec = pl.BlockSpec((PEER_HEADS, 1, i1c, tt), lambda i, j: (0, prev(j), 0, i))
    return pl.pallas_call(
        kern,
        grid=(T // tt, n_j + 1),
        in_specs=[
            pl.BlockSpec((tt, D), lambda i, j: (i, 0)),
            pl.BlockSpec((tt, D), lambda i, j: (i, 0)),
            pl.BlockSpec((ec, D), lambda i, j: (cur(j), 0)),
            pl.BlockSpec((1, D, ec), lambda i, j: (prev(j), 0, 0)),
            re_spec, row_spec, row_spec,
            pl.BlockSpec((1, D), lambda i, j: (0, 0)),
        ],
        out_specs=pl.BlockSpec((tt, D), lambda i, j: (i, 0)),
        out_shape=jax.ShapeDtypeStruct((T, D), F32),
        scratch_shapes=[
            pltpu.VMEM((D, tt), F32),
            pltpu.VMEM((ec, tt), F32),
            pltpu.VMEM((ec, tt), F32),
            pltpu.VMEM((ec, min(tt, MXU_DIM)), BF16),
        ],
        compiler_params=pltpu.CompilerParams(
            dimension_semantics=("arbitrary", "arbitrary"), vmem_limit_bytes=VMEM_LIMIT),
        name="peer_experts",
    )(xn, x, u_bf, vt_bf, re, n1, e1, g_final.reshape(1, D))


def _split_param(a):
    hi = a.astype(BF16)
    return hi, (a - hi.astype(F32)).astype(BF16)


def _trunk(x, pool_hist, conv_hist, pos0, p):
    B, S, D = x.shape
    x, new_pool = _pool_layer(x, pool_hist, pos0, p["norm_mix"][0], p["pool_w_hi"], p["pool_w_lo"], p["pool_scale"][0])
    for i in range(2):
        if i == 1:
            x, new_conv = _conv_layer(x, conv_hist, p["norm_mix"][1], p["conv_w_in"][0], p["conv_b_in"][0],
                                      p["conv_dw"][0], p["conv_dw_b"][0], p["conv_ln_g"][0], p["conv_ln_b"][0],
                                      p["conv_w_out"][0], p["conv_b_out"][0])
        xt = x.reshape(B * S, D)
        xn, re, n1, e1 = _peer_route(xt, p["norm_ffn"][i], *p["route"][i])
        xt = _peer_experts(xt, xn, p["u_bf"][i], p["vt_bf"][i], re, n1, e1, p["norm_final"], final=(i == 1))
        x = xt.reshape(B, S, D)
    return x, new_pool[None], new_conv[None]


def kernel(x_prompt, x_sample, cache_pool, state_conv, norm_mix, norm_ffn, norm_final, pool_w, pool_scale, conv_w_in, conv_b_in, conv_dw, conv_dw_b, conv_ln_g, conv_ln_b, conv_w_out, conv_b_out, peer_wq, peer_keys, peer_u, peer_v):
    B, _, D = x_prompt.shape
    depth = norm_mix.shape[0]
    assert depth == 2 and pool_w.shape[0] == 1 and conv_w_in.shape[0] == 1
    assert peer_keys.shape[1:] == (PEER_HEADS, 2, PEER_NKEYS, PEER_HALF)

    pool_w_hi, pool_w_lo = _split_param(pool_w[0])
    route = []
    for i in range(depth):
        wq_hi, wq_lo = _split_param(peer_wq[i].T)
        k1_hi, k1_lo = _split_param(peer_keys[i, :, 0])
        k2_hi, k2_lo = _split_param(peer_keys[i, :, 1])
        route.append((wq_hi, wq_lo, k1_hi, k1_lo, k2_hi, k2_lo))
    p = dict(
        norm_mix=norm_mix, norm_ffn=norm_ffn, norm_final=norm_final,
        pool_w_hi=pool_w_hi, pool_w_lo=pool_w_lo, pool_scale=pool_scale,
        conv_w_in=conv_w_in, conv_b_in=conv_b_in, conv_dw=conv_dw, conv_dw_b=conv_dw_b,
        conv_ln_g=conv_ln_g, conv_ln_b=conv_ln_b, conv_w_out=conv_w_out, conv_b_out=conv_b_out,
        route=route,
        u_bf=[peer_u[i].astype(BF16) for i in range(depth)],
        vt_bf=[peer_v[i].astype(BF16).reshape(-1, EXPERT_CHUNK, D).transpose(0, 2, 1) for i in range(depth)],
    )
    zero_pool = jnp.zeros((B, POOL_HIST, D), x_prompt.dtype)
    zero_conv = jnp.zeros((B, CONV_HIST, D), x_prompt.dtype)
    y_p, np_p, nc_p = _trunk(x_prompt, zero_pool, zero_conv, 0, p)
    past_len = 4096
    y_s, np_s, nc_s = _trunk(x_sample, cache_pool[0], state_conv[0], past_len, p)
    return (y_p, y_s, np_p, nc_p, np_s, nc_s)
```

```python
import functools
import math

import jax
import jax.numpy as jnp
from jax import lax
from jax.experimental import pallas as pl
from jax.experimental.pallas import tpu as pltpu

EPS = 1e-6
POOL_WINDOWS = (2, 4, 8, 16)
POOL_HIST = 15
POOL_PAD = 16
CONV_WIDTH = 31
CONV_HIST = 30
CONV_PAD = 32
PEER_HEADS = 8
PEER_NKEYS = 128
PEER_HALF = 64
PEER_TOPK = 16
RANK_SCALE = 1.0 / 32.0
LANES = 128
BF16_SUBLANES = 16
MXU_DIM = 256
EXPERT_CHUNK = 1024
VMEM_LIMIT = 56 * 1024 * 1024

F32 = jnp.float32
BF16 = jnp.bfloat16


def _rmsnorm(x, g):
    ms = jnp.mean(x * x, axis=-1, keepdims=True)
    return x * lax.rsqrt(ms + EPS) * g


def _split_bf16(a):
    hi = a.astype(BF16)
    lo = (a - hi.astype(F32)).astype(BF16)
    return hi, lo


def _dot(a, b):
    return jnp.dot(a, b, preferred_element_type=F32)


def _dot3(a_hi, a_lo, b_hi, b_lo):
    return _dot(a_hi, b_hi) + _dot(a_lo, b_hi) + _dot(a_hi, b_lo)


def _pool_kernel(x_ref, hist_ref, g_ref, wh_ref, wl_ref, sc_ref, o_ref, np_ref, xp_ref, *, ts, pos0, n_s, gw):
    s = pl.program_id(1)

    @pl.when(s == 0)
    def _():
        xp_ref[0:POOL_PAD, :] = hist_ref[0]

    @pl.when(s > 0)
    def _():
        xp_ref[0:POOL_PAD, :] = xp_ref[ts:ts + POOL_PAD, :]

    x = x_ref[0]
    xn = _rmsnorm(x, g_ref[...])
    xp_ref[POOL_PAD:POOL_PAD + ts, :] = xn

    for g, win in enumerate(POOL_WINDOWS):
        c0 = g * gw
        xg = xn[:, c0:c0 + gw]
        acc = xg
        for j in range(1, win):
            acc = acc + xp_ref[POOL_PAD - j:POOL_PAD - j + ts, c0:c0 + gw]
        if pos0 >= win - 1:
            mean = acc * (1.0 / win)
        else:
            pos = pos0 + s * ts + lax.broadcasted_iota(jnp.int32, (ts, gw), 0)
            cnt = jnp.minimum(win, pos + 1).astype(F32)
            mean = acc / cnt
        d_hi, d_lo = _split_bf16(mean - xg)
        y = _dot3(d_hi, d_lo, wh_ref[g], wl_ref[g])
        o_ref[0, :, c0:c0 + gw] = x[:, c0:c0 + gw] + y * sc_ref[:, c0:c0 + gw]

    @pl.when(s == n_s - 1)
    def _():
        np_ref[0] = xp_ref[ts:ts + POOL_PAD, :]


def _pool_layer(x, hist, pos0, g, w_hi, w_lo, scale):
    B, S, D = x.shape
    ts = min(S, 512)
    n_s = S // ts
    gw = D // len(POOL_WINDOWS)
    hist_p = jnp.pad(hist, ((0, 0), (POOL_PAD - POOL_HIST, 0), (0, 0)))
    kern = functools.partial(_pool_kernel, ts=ts, pos0=pos0, n_s=n_s, gw=gw)
    out, new_hist = pl.pallas_call(
        kern,
        grid=(B, n_s),
        in_specs=[
            pl.BlockSpec((1, ts, D), lambda b, s: (b, s, 0)),
            pl.BlockSpec((1, POOL_PAD, D), lambda b, s: (b, 0, 0)),
            pl.BlockSpec((1, D), lambda b, s: (0, 0)),
            pl.BlockSpec(w_hi.shape, lambda b, s: (0, 0, 0)),
            pl.BlockSpec(w_lo.shape, lambda b, s: (0, 0, 0)),
            pl.BlockSpec((1, D), lambda b, s: (0, 0)),
        ],
        out_specs=[
            pl.BlockSpec((1, ts, D), lambda b, s: (b, s, 0)),
            pl.BlockSpec((1, POOL_PAD, D), lambda b, s: (b, 0, 0)),
        ],
        out_shape=[
            jax.ShapeDtypeStruct((B, S, D), F32),
            jax.ShapeDtypeStruct((B, POOL_PAD, D), F32),
        ],
        scratch_shapes=[pltpu.VMEM((POOL_PAD + ts, D), F32)],
        compiler_params=pltpu.CompilerParams(
            dimension_semantics=("arbitrary", "arbitrary"), vmem_limit_bytes=VMEM_LIMIT),
        name="pool_layer",
    )(x, hist_p, g.reshape(1, D), w_hi, w_lo, scale.reshape(1, D))
    return out, new_hist[:, POOL_PAD - POOL_HIST:]


def _conv_kernel(x_ref, hist_ref, g_ref, win_ref, bin_ref, dw_ref, dwb_ref, lng_ref, lnb_ref, wout_ref, bout_ref,
                 o_ref, nc_ref, up_ref, z_ref, *, ts, n_s, rb):
    s = pl.program_id(1)
    D = x_ref.shape[-1]

    @pl.when(s == 0)
    def _():
        up_ref[0:CONV_PAD, :] = hist_ref[0]

    @pl.when(s > 0)
    def _():
        up_ref[0:CONV_PAD, :] = up_ref[ts:ts + CONV_PAD, :]

    x = x_ref[0]
    xn = _rmsnorm(x, g_ref[...])
    h = _dot(xn.astype(BF16), win_ref[...]) + bin_ref[...]
    u = h[:, :D] * jax.nn.sigmoid(h[:, D:])
    up_ref[CONV_PAD:CONV_PAD + ts, :] = u

    base = CONV_PAD - CONV_HIST
    for r0 in range(0, ts, rb):
        for c0 in range(0, D, LANES):
            acc = jnp.zeros((rb, LANES), F32) + dwb_ref[:, c0:c0 + LANES]
            for k in range(CONV_WIDTH):
                acc = acc + up_ref[base + r0 + k:base + r0 + k + rb, c0:c0 + LANES] * dw_ref[k:k + 1, c0:c0 + LANES]
            z_ref[r0:r0 + rb, c0:c0 + LANES] = acc

    z = z_ref[...]
    mu = jnp.mean(z, axis=-1, keepdims=True)
    zc = z - mu
    var = jnp.mean(zc * zc, axis=-1, keepdims=True)
    zn = zc * lax.rsqrt(var + EPS) * lng_ref[...] + lnb_ref[...]
    act = zn * jax.nn.sigmoid(zn)
    y = _dot(act.astype(BF16), wout_ref[...]) + bout_ref[...]
    o_ref[0] = x + y

    @pl.when(s == n_s - 1)
    def _():
        nc_ref[0] = up_ref[ts:ts + CONV_PAD, :]


def _conv_layer(x, hist, g, w_in, b_in, dw, dw_b, ln_g, ln_b, w_out, b_out):
    B, S, D = x.shape
    ts = min(S, 256)
    n_s = S // ts
    hist_p = jnp.pad(hist, ((0, 0), (CONV_PAD - CONV_HIST, 0), (0, 0)))
    kern = functools.partial(_conv_kernel, ts=ts, n_s=n_s, rb=min(ts, 64))
    row = lambda a: a.reshape(1, -1)
    const2 = lambda b, s: (0, 0)
    out, new_hist = pl.pallas_call(
        kern,
        grid=(B, n_s),
        in_specs=[
            pl.BlockSpec((1, ts, D), lambda b, s: (b, s, 0)),
            pl.BlockSpec((1, CONV_PAD, D), lambda b, s: (b, 0, 0)),
            pl.BlockSpec((1, D), const2),
            pl.BlockSpec((D, 2 * D), const2),
            pl.BlockSpec((1, 2 * D), const2),
            pl.BlockSpec((CONV_WIDTH, D), const2),
            pl.BlockSpec((1, D), const2),
            pl.BlockSpec((1, D), const2),
            pl.BlockSpec((1, D), const2),
            pl.BlockSpec((D, D), const2),
            pl.BlockSpec((1, D), const2),
        ],
        out_specs=[
            pl.BlockSpec((1, ts, D), lambda b, s: (b, s, 0)),
            pl.BlockSpec((1, CONV_PAD, D), lambda b, s: (b, 0, 0)),
        ],
        out_shape=[
            jax.ShapeDtypeStruct((B, S, D), F32),
            jax.ShapeDtypeStruct((B, CONV_PAD, D), F32),
        ],
        scratch_shapes=[pltpu.VMEM((CONV_PAD + ts, D), F32), pltpu.VMEM((ts, D), F32)],
        compiler_params=pltpu.CompilerParams(
            dimension_semantics=("arbitrary", "arbitrary"), vmem_limit_bytes=VMEM_LIMIT),
        name="conv_layer",
    )(x, hist_p, row(g), w_in.astype(BF16), row(b_in), dw, row(dw_b), row(ln_g), row(ln_b),
      w_out.astype(BF16), row(b_out))
    return out, new_hist[:, CONV_PAD - CONV_HIST:]


def _top16(s, v_ref):
    work = s
    rank = jnp.full(s.shape, float(PEER_TOPK), F32)
    for r in range(PEER_TOPK):
        m = jnp.max(work, axis=0, keepdims=True)
        v_ref[r:r + 1, :] = m
        hit = work == m
        rank = jnp.where(hit, float(r), rank)
        work = jnp.where(hit, -jnp.inf, work)
    return rank


def _pair_select(v1, v2):
    row = lax.broadcasted_iota(jnp.int32, (8, LANES), 0)
    neg = -jnp.inf
    cands = [v1[0:1] + v2[0:8], v1[0:1] + v2[8:16], v1[1:2] + v2[0:8]]
    for a in range(2, 8):
        cands.append(jnp.where(row < PEER_TOPK // (a + 1), v1[a:a + 1] + v2[0:8], neg))
    cands.append(v1[8:16] + v2[0:1])

    work = list(cands)
    vals = []
    for _ in range(PEER_TOPK):
        m8 = work[0]
        for w in work[1:]:
            m8 = jnp.maximum(m8, w)
        m = jnp.max(m8, axis=0, keepdims=True)
        vals.append(m)
        work = [jnp.where(w == m, neg, w) for w in work]
    tau, top = vals[-1], vals[0]
    z = jnp.ones_like(top)
    for v in vals[1:]:
        z = z + jnp.exp(v - top)

    sel = [(c >= tau).astype(F32) for c in cands]
    colsum = lambda a: jnp.sum(a, axis=0, keepdims=True)
    n_lo = [colsum(sel[0]) + colsum(sel[1])] + [colsum(sel[i]) for i in range(2, 9)]
    n_hi = sel[9]
    return n_lo, n_hi, z


def _route_kernel(x_ref, g_ref, wqh_ref, wql_ref, k1h_ref, k1l_ref, k2h_ref, k2l_ref,
                  xnt_ref, re_ref, n1_ref, e1_ref, qt_ref, s_ref, v1_ref, v2_ref, *, tt):
    xt_hi, xt_lo = _split_bf16(_rmsnorm(x_ref[...], g_ref[...]).T)
    xnt_ref[...] = xt_hi
    qt_ref[...] = _dot3(wqh_ref[...], wql_ref[...], xt_hi, xt_lo)

    def head_body(h, carry):
        q = qt_ref[pl.ds(pl.multiple_of(h * (2 * PEER_HALF), 2 * PEER_HALF), 2 * PEER_HALF), :]
        q1_hi, q1_lo = _split_bf16(q[:PEER_HALF])
        q2_hi, q2_lo = _split_bf16(q[PEER_HALF:])
        s1 = _dot3(k1h_ref[h], k1l_ref[h], q1_hi, q1_lo)
        s2 = _dot3(k2h_ref[h], k2l_ref[h], q2_hi, q2_lo)
        for c in range(tt // LANES):
            s_ref[0, c] = s1[:, c * LANES:(c + 1) * LANES]
            s_ref[1, c] = s2[:, c * LANES:(c + 1) * LANES]
        for c in range(tt // LANES):
            cols = slice(c * LANES, (c + 1) * LANES)
            rank1 = _top16(s_ref[0, c], v1_ref)
            rank2 = _top16(s_ref[1, c], v2_ref)
            v1 = v1_ref[...]
            v2 = v2_ref[...]
            r2 = (rank2 * RANK_SCALE).astype(BF16)
            e2 = jnp.exp(s_ref[1, c] - v2[0:1]).astype(BF16)
            for sub in range(PEER_NKEYS // BF16_SUBLANES):
                rows = slice(sub * BF16_SUBLANES, (sub + 1) * BF16_SUBLANES)
                re_ref[c, sub, h, 0] = r2[rows]
                re_ref[c, sub, h, 1] = e2[rows]
            n_lo, n_hi, z = _pair_select(v1, v2)
            n1 = jnp.zeros((PEER_NKEYS, LANES), F32)
            for a in range(PEER_TOPK):
                n_a = n_lo[a] if a < 8 else n_hi[a - 8:a - 7]
                n1 = jnp.where(rank1 == float(a), n_a, n1)
            n1_ref[h, :, cols] = n1 * RANK_SCALE + (1.0 - RANK_SCALE)
            e1_ref[h, :, cols] = jnp.exp(s_ref[0, c] - v1[0:1]) * (1.0 / z)
        return carry

    lax.fori_loop(0, PEER_HEADS, head_body, 0)


def _re_shape(n_tokens):
    return (n_tokens // LANES, PEER_NKEYS // BF16_SUBLANES, PEER_HEADS, 2, BF16_SUBLANES, LANES)


def _peer_route(x, g, wq_hi, wq_lo, k1_hi, k1_lo, k2_hi, k2_lo):
    T, D = x.shape
    tt = min(T, 512)
    hd = wq_hi.shape[0]
    kern = functools.partial(_route_kernel, tt=tt)
    c2 = lambda i: (0, 0)
    c3 = lambda i: (0, 0, 0)
    tabf = jax.ShapeDtypeStruct((PEER_HEADS, PEER_NKEYS, T), F32)
    tab_spec = pl.BlockSpec((PEER_HEADS, PEER_NKEYS, tt), lambda i: (0, 0, i))
    re_shape = _re_shape(T)
    re_spec = pl.BlockSpec(_re_shape(tt), lambda i: (i, 0, 0, 0, 0, 0))
    return pl.pallas_call(
        kern,
        grid=(T // tt,),
        in_specs=[
            pl.BlockSpec((tt, D), lambda i: (i, 0)),
            pl.BlockSpec((1, D), c2),
            pl.BlockSpec((hd, D), c2),
            pl.BlockSpec((hd, D), c2),
            pl.BlockSpec(k1_hi.shape, c3),
            pl.BlockSpec(k1_hi.shape, c3),
            pl.BlockSpec(k1_hi.shape, c3),
            pl.BlockSpec(k1_hi.shape, c3),
        ],
        out_specs=[pl.BlockSpec((D, tt), lambda i: (0, i)), re_spec, tab_spec, tab_spec],
        out_shape=[jax.ShapeDtypeStruct((D, T), BF16), jax.ShapeDtypeStruct(re_shape, BF16), tabf, tabf],
        scratch_shapes=[
            pltpu.VMEM((hd, tt), F32),
            pltpu.VMEM((2, tt // LANES, PEER_NKEYS, LANES), F32),
            pltpu.VMEM((PEER_TOPK, LANES), F32),
            pltpu.VMEM((PEER_TOPK, LANES), F32),
        ],
        compiler_params=pltpu.CompilerParams(
            dimension_semantics=("arbitrary",), vmem_limit_bytes=VMEM_LIMIT),
        name="peer_route",
    )(x, g.reshape(1, D), wq_hi, wq_lo, k1_hi, k1_lo, k2_hi, k2_lo)


def _gelu(h):
    return 0.5 * h * (1.0 + lax.erf(h * (1.0 / math.sqrt(2.0))))


def _row_bf16(ref, h, k, cols):
    return jnp.broadcast_to(ref[h, 0, k:k + 1, cols], (BF16_SUBLANES, LANES)).astype(BF16)


def _expert_step(xnt_ref, u_ref, vt_ref, re_ref, n1_ref, e1_ref, acc_ref, w_ref, hid_w, hid_r, *, tt, ec):
    tw = min(tt, MXU_DIM)
    kpt = MXU_DIM // PEER_NKEYS

    def token_tile(n, carry):
        t0 = pl.multiple_of(n * tw, tw)
        tok = pl.ds(t0, tw)
        hid_w[:, tok] = _dot(u_ref[...], xnt_ref[:, tok])

        part = None
        for kt in range(ec // MXU_DIM):
            ks = range(kt * kpt, (kt + 1) * kpt)
            for cc in range(tw // LANES):
                c = n * (tw // LANES) + cc
                cols = pl.ds(pl.multiple_of(t0 + cc * LANES, LANES), LANES)
                wcols = slice(cc * LANES, (cc + 1) * LANES)
                n1 = [[_row_bf16(n1_ref, h, k, cols) for h in range(PEER_HEADS)] for k in ks]
                e1 = [[_row_bf16(e1_ref, h, k, cols) for h in range(PEER_HEADS)] for k in ks]
                for sub in range(PEER_NKEYS // BF16_SUBLANES):
                    gates = [None] * kpt
                    for h in range(PEER_HEADS):
                        r2 = re_ref[c, sub, h, 0]
                        e2 = re_ref[c, sub, h, 1]
                        for q in range(kpt):
                            g_h = (e1[q][h] * jnp.floor(n1[q][h] - r2)) * e2
                            gates[q] = g_h if gates[q] is None else gates[q] + g_h
                    for q, k in enumerate(ks):
                        r0 = k * PEER_NKEYS + sub * BF16_SUBLANES
                        rows = slice(r0, r0 + BF16_SUBLANES)
                        w_ref[rows, wcols] = gates[q] * _gelu(hid_r[rows, cols]).astype(BF16)
            kk = slice(kt * MXU_DIM, (kt + 1) * MXU_DIM)
            p = _dot(vt_ref[0, :, kk], w_ref[kk, :])
            part = p if part is None else part + p
        acc_ref[:, tok] += part
        return carry

    lax.fori_loop(0, tt // tw, token_tile, 0)


def _expert_kernel(xnt_ref, x_ref, u_ref, vt_ref, re_ref, n1_ref, e1_ref, gf_ref,
                   o_ref, acc_ref, hid_0, hid_1, w_ref, *, tt, ec, n_j, final):
    g = pl.program_id(0)
    b = g - 1
    jb = jnp.maximum(b, 0) % n_j
    step = functools.partial(_expert_step, xnt_ref, u_ref, vt_ref, re_ref, n1_ref, e1_ref, acc_ref, w_ref,
                             tt=tt, ec=ec)

    @pl.when(g == 0)
    def _():
        hid_1[...] = jnp.zeros_like(hid_1)

    @pl.when(jb == 0)
    def _():
        acc_ref[...] = jnp.zeros_like(acc_ref)

    @pl.when(g % 2 == 0)
    def _():
        step(hid_0, hid_1)

    @pl.when(g % 2 == 1)
    def _():
        step(hid_1, hid_0)

    @pl.when((jb == n_j - 1) & (b >= 0))
    def _():
        out = x_ref[...] + acc_ref[...].T
        if final:
            out = _rmsnorm(out, gf_ref[...])
        o_ref[...] = out


def _peer_experts(x, xnt, u_bf, vt_bf, re, n1, e1, g_final, final):
    T, D = x.shape
    n_exp = u_bf.shape[0]
    tt = min(T, 512)
    ec = EXPERT_CHUNK
    n_j = n_exp // ec
    assert n_j > 1 and ec % MXU_DIM == 0
    i1c = ec // PEER_NKEYS
    n1 = n1.reshape(PEER_HEADS, PEER_NKEYS // i1c, i1c, T)
    e1 = e1.reshape(PEER_HEADS, PEER_NKEYS // i1c, i1c, T)
    n_flat = (T // tt) * n_j
    kern = functools.partial(_expert_kernel, tt=tt, ec=ec, n_j=n_j, final=final)

    def stage(lag):
        flat = lambda g: jnp.clip(g - lag, 0, n_flat - 1)
        return (lambda g: flat(g) // n_j), (lambda g: flat(g) % n_j)

    tile_a, chunk_a = stage(0)
    tile_b, chunk_b = stage(1)
    row_spec = pl.BlockSpec((PEER_HEADS, 1, i1c, tt), lambda g: (0, chunk_b(g), 0, tile_b(g)))
    return pl.pallas_call(
        kern,
        grid=(n_flat + 1,),
        in_specs=[
            pl.BlockSpec((D, tt), lambda g: (0, tile_a(g))),
            pl.BlockSpec((tt, D), lambda g: (tile_b(g), 0)),
            pl.BlockSpec((ec, D), lambda g: (chunk_a(g), 0)),
            pl.BlockSpec((1, D, ec), lambda g: (chunk_b(g), 0, 0)),
            pl.BlockSpec(_re_shape(tt), lambda g: (tile_b(g), 0, 0, 0, 0, 0)),
            row_spec, row_spec,
            pl.BlockSpec((1, D), lambda g: (0, 0)),
        ],
        out_specs=pl.BlockSpec((tt, D), lambda g: (tile_b(g), 0)),
        out_shape=jax.ShapeDtypeStruct((T, D), F32),
        scratch_shapes=[
            pltpu.VMEM((D, tt), F32),
            pltpu.VMEM((ec, tt), F32),
            pltpu.VMEM((ec, tt), F32),
            pltpu.VMEM((ec, min(tt, MXU_DIM)), BF16),
        ],
        compiler_params=pltpu.CompilerParams(
            dimension_semantics=("arbitrary",), vmem_limit_bytes=VMEM_LIMIT),
        name="peer_experts",
    )(xnt, x, u_bf, vt_bf, re, n1, e1, g_final.reshape(1, D))


def _split_param(a):
    hi = a.astype(BF16)
    return hi, (a - hi.astype(F32)).astype(BF16)


def _trunk(x, pool_hist, conv_hist, pos0, p):
    B, S, D = x.shape
    x, new_pool = _pool_layer(x, pool_hist, pos0, p["norm_mix"][0], p["pool_w_hi"], p["pool_w_lo"], p["pool_scale"][0])
    for i in range(2):
        if i == 1:
            x, new_conv = _conv_layer(x, conv_hist, p["norm_mix"][1], p["conv_w_in"][0], p["conv_b_in"][0],
                                      p["conv_dw"][0], p["conv_dw_b"][0], p["conv_ln_g"][0], p["conv_ln_b"][0],
                                      p["conv_w_out"][0], p["conv_b_out"][0])
        xt = x.reshape(B * S, D)
        xnt, re, n1, e1 = _peer_route(xt, p["norm_ffn"][i], *p["route"][i])
        xt = _peer_experts(xt, xnt, p["u_bf"][i], p["vt_bf"][i], re, n1, e1, p["norm_final"], final=(i == 1))
        x = xt.reshape(B, S, D)
    return x, new_pool[None], new_conv[None]


def kernel(x_prompt, x_sample, cache_pool, state_conv, norm_mix, norm_ffn, norm_final, pool_w, pool_scale, conv_w_in, conv_b_in, conv_dw, conv_dw_b, conv_ln_g, conv_ln_b, conv_w_out, conv_b_out, peer_wq, peer_keys, peer_u, peer_v):
    B, _, D = x_prompt.shape
    depth = norm_mix.shape[0]
    assert depth == 2 and pool_w.shape[0] == 1 and conv_w_in.shape[0] == 1
    assert peer_keys.shape[1:] == (PEER_HEADS, 2, PEER_NKEYS, PEER_HALF)

    pool_w_hi, pool_w_lo = _split_param(pool_w[0])
    route = []
    for i in range(depth):
        wq_hi, wq_lo = _split_param(peer_wq[i].T)
        k1_hi, k1_lo = _split_param(peer_keys[i, :, 0])
        k2_hi, k2_lo = _split_param(peer_keys[i, :, 1])
        route.append((wq_hi, wq_lo, k1_hi, k1_lo, k2_hi, k2_lo))
    p = dict(
        norm_mix=norm_mix, norm_ffn=norm_ffn, norm_final=norm_final,
        pool_w_hi=pool_w_hi, pool_w_lo=pool_w_lo, pool_scale=pool_scale,
        conv_w_in=conv_w_in, conv_b_in=conv_b_in, conv_dw=conv_dw, conv_dw_b=conv_dw_b,
        conv_ln_g=conv_ln_g, conv_ln_b=conv_ln_b, conv_w_out=conv_w_out, conv_b_out=conv_b_out,
        route=route,
        u_bf=[peer_u[i].astype(BF16) for i in range(depth)],
        vt_bf=[peer_v[i].astype(BF16).reshape(-1, EXPERT_CHUNK, D).transpose(0, 2, 1) for i in range(depth)],
    )
    zero_pool = jnp.zeros((B, POOL_HIST, D), x_prompt.dtype)
    zero_conv = jnp.zeros((B, CONV_HIST, D), x_prompt.dtype)
    y_p, np_p, nc_p = _trunk(x_prompt, zero_pool, zero_conv, 0, p)
    past_len = 4096
    y_s, np_s, nc_s = _trunk(x_sample, cache_pool[0], state_conv[0], past_len, p)
    return (y_p, y_s, np_p, nc_p, np_s, nc_s)
```

```python
import functools
import math

import jax
import jax.numpy as jnp
from jax import lax
from jax.experimental import pallas as pl
from jax.experimental.pallas import tpu as pltpu

EPS = 1e-6
POOL_WINDOWS = (2, 4, 8, 16)
POOL_HIST = 15
POOL_PAD = 16
CONV_WIDTH = 31
CONV_HIST = 30
CONV_PAD = 32
PEER_HEADS = 8
PEER_NKEYS = 128
PEER_HALF = 64
PEER_TOPK = 16
RANK_SCALE = 1.0 / 32.0
LANES = 128
SUBLANES = 8
BF16_SUBLANES = 16
MXU_DIM = 256
EXPERT_CHUNK = 1024
VMEM_LIMIT = 56 * 1024 * 1024

F32 = jnp.float32
BF16 = jnp.bfloat16


def _rmsnorm(x, g):
    ms = jnp.mean(x * x, axis=-1, keepdims=True)
    return x * lax.rsqrt(ms + EPS) * g


def _split_bf16(a):
    hi = a.astype(BF16)
    lo = (a - hi.astype(F32)).astype(BF16)
    return hi, lo


def _dot(a, b):
    return jnp.dot(a, b, preferred_element_type=F32)


def _dot3(a_hi, a_lo, b_hi, b_lo):
    return _dot(a_hi, b_hi) + _dot(a_lo, b_hi) + _dot(a_hi, b_lo)


def _pool_kernel(x_ref, hist_ref, g_ref, wh_ref, wl_ref, sc_ref, o_ref, np_ref, xp_ref, *, ts, pos0, n_s, gw):
    s = pl.program_id(1)

    @pl.when(s == 0)
    def _():
        xp_ref[0:POOL_PAD, :] = hist_ref[0]

    @pl.when(s > 0)
    def _():
        xp_ref[0:POOL_PAD, :] = xp_ref[ts:ts + POOL_PAD, :]

    x = x_ref[0]
    xn = _rmsnorm(x, g_ref[...])
    xp_ref[POOL_PAD:POOL_PAD + ts, :] = xn

    for g, win in enumerate(POOL_WINDOWS):
        c0 = g * gw
        xg = xn[:, c0:c0 + gw]
        acc = xg
        for j in range(1, win):
            acc = acc + xp_ref[POOL_PAD - j:POOL_PAD - j + ts, c0:c0 + gw]
        if pos0 >= win - 1:
            mean = acc * (1.0 / win)
        else:
            pos = pos0 + s * ts + lax.broadcasted_iota(jnp.int32, (ts, gw), 0)
            cnt = jnp.minimum(win, pos + 1).astype(F32)
            mean = acc / cnt
        d_hi, d_lo = _split_bf16(mean - xg)
        y = _dot3(d_hi, d_lo, wh_ref[g], wl_ref[g])
        o_ref[0, :, c0:c0 + gw] = x[:, c0:c0 + gw] + y * sc_ref[:, c0:c0 + gw]

    @pl.when(s == n_s - 1)
    def _():
        np_ref[0] = xp_ref[ts:ts + POOL_PAD, :]


def _pool_layer(x, hist, pos0, g, w_hi, w_lo, scale):
    B, S, D = x.shape
    ts = min(S, 512)
    n_s = S // ts
    gw = D // len(POOL_WINDOWS)
    hist_p = jnp.pad(hist, ((0, 0), (POOL_PAD - POOL_HIST, 0), (0, 0)))
    kern = functools.partial(_pool_kernel, ts=ts, pos0=pos0, n_s=n_s, gw=gw)
    out, new_hist = pl.pallas_call(
        kern,
        grid=(B, n_s),
        in_specs=[
            pl.BlockSpec((1, ts, D), lambda b, s: (b, s, 0)),
            pl.BlockSpec((1, POOL_PAD, D), lambda b, s: (b, 0, 0)),
            pl.BlockSpec((1, D), lambda b, s: (0, 0)),
            pl.BlockSpec(w_hi.shape, lambda b, s: (0, 0, 0)),
            pl.BlockSpec(w_lo.shape, lambda b, s: (0, 0, 0)),
            pl.BlockSpec((1, D), lambda b, s: (0, 0)),
        ],
        out_specs=[
            pl.BlockSpec((1, ts, D), lambda b, s: (b, s, 0)),
            pl.BlockSpec((1, POOL_PAD, D), lambda b, s: (b, 0, 0)),
        ],
        out_shape=[
            jax.ShapeDtypeStruct((B, S, D), F32),
            jax.ShapeDtypeStruct((B, POOL_PAD, D), F32),
        ],
        scratch_shapes=[pltpu.VMEM((POOL_PAD + ts, D), F32)],
        compiler_params=pltpu.CompilerParams(
            dimension_semantics=("arbitrary", "arbitrary"), vmem_limit_bytes=VMEM_LIMIT),
        name="pool_layer",
    )(x, hist_p, g.reshape(1, D), w_hi, w_lo, scale.reshape(1, D))
    return out, new_hist[:, POOL_PAD - POOL_HIST:]


def _conv_kernel(x_ref, hist_ref, g_ref, win_ref, bin_ref, dw_ref, dwb_ref, lng_ref, lnb_ref, wout_ref, bout_ref,
                 o_ref, nc_ref, up_ref, z_ref, *, ts, n_s, rb):
    s = pl.program_id(1)
    D = x_ref.shape[-1]

    @pl.when(s == 0)
    def _():
        up_ref[0:CONV_PAD, :] = hist_ref[0]

    @pl.when(s > 0)
    def _():
        up_ref[0:CONV_PAD, :] = up_ref[ts:ts + CONV_PAD, :]

    x = x_ref[0]
    xn = _rmsnorm(x, g_ref[...])
    h = _dot(xn.astype(BF16), win_ref[...]) + bin_ref[...]
    u = h[:, :D] * jax.nn.sigmoid(h[:, D:])
    up_ref[CONV_PAD:CONV_PAD + ts, :] = u

    base = CONV_PAD - CONV_HIST
    for r0 in range(0, ts, rb):
        for c0 in range(0, D, LANES):
            cols = slice(c0, c0 + LANES)
            win = up_ref[r0:r0 + rb + CONV_PAD, cols]
            acc = jnp.zeros((rb, LANES), F32) + dwb_ref[:, cols]
            for shift in range(SUBLANES):
                taps = [k for k in range(CONV_WIDTH) if (base + k) % SUBLANES == shift]
                reach = max(base + k - shift for k in taps) + rb
                xs = win[shift:shift + reach]
                for k in taps:
                    a = base + k - shift
                    acc = acc + xs[a:a + rb] * dw_ref[k:k + 1, cols]
            z_ref[r0:r0 + rb, cols] = acc

    z = z_ref[...]
    mu = jnp.mean(z, axis=-1, keepdims=True)
    zc = z - mu
    var = jnp.mean(zc * zc, axis=-1, keepdims=True)
    zn = zc * lax.rsqrt(var + EPS) * lng_ref[...] + lnb_ref[...]
    act = zn * jax.nn.sigmoid(zn)
    y = _dot(act.astype(BF16), wout_ref[...]) + bout_ref[...]
    o_ref[0] = x + y

    @pl.when(s == n_s - 1)
    def _():
        nc_ref[0] = up_ref[ts:ts + CONV_PAD, :]


def _conv_layer(x, hist, g, w_in, b_in, dw, dw_b, ln_g, ln_b, w_out, b_out):
    B, S, D = x.shape
    ts = min(S, 256)
    n_s = S // ts
    hist_p = jnp.pad(hist, ((0, 0), (CONV_PAD - CONV_HIST, 0), (0, 0)))
    kern = functools.partial(_conv_kernel, ts=ts, n_s=n_s, rb=min(ts, 64))
    row = lambda a: a.reshape(1, -1)
    const2 = lambda b, s: (0, 0)
    out, new_hist = pl.pallas_call(
        kern,
        grid=(B, n_s),
        in_specs=[
            pl.BlockSpec((1, ts, D), lambda b, s: (b, s, 0)),
            pl.BlockSpec((1, CONV_PAD, D), lambda b, s: (b, 0, 0)),
            pl.BlockSpec((1, D), const2),
            pl.BlockSpec((D, 2 * D), const2),
            pl.BlockSpec((1, 2 * D), const2),
            pl.BlockSpec((CONV_WIDTH, D), const2),
            pl.BlockSpec((1, D), const2),
            pl.BlockSpec((1, D), const2),
            pl.BlockSpec((1, D), const2),
            pl.BlockSpec((D, D), const2),
            pl.BlockSpec((1, D), const2),
        ],
        out_specs=[
            pl.BlockSpec((1, ts, D), lambda b, s: (b, s, 0)),
            pl.BlockSpec((1, CONV_PAD, D), lambda b, s: (b, 0, 0)),
        ],
        out_shape=[
            jax.ShapeDtypeStruct((B, S, D), F32),
            jax.ShapeDtypeStruct((B, CONV_PAD, D), F32),
        ],
        scratch_shapes=[pltpu.VMEM((CONV_PAD + ts, D), F32), pltpu.VMEM((ts, D), F32)],
        compiler_params=pltpu.CompilerParams(
            dimension_semantics=("arbitrary", "arbitrary"), vmem_limit_bytes=VMEM_LIMIT),
        name="conv_layer",
    )(x, hist_p, row(g), w_in.astype(BF16), row(b_in), dw, row(dw_b), row(ln_g), row(ln_b),
      w_out.astype(BF16), row(b_out))
    return out, new_hist[:, CONV_PAD - CONV_HIST:]


def _top16(s, v_ref):
    work = s
    rank = jnp.full(s.shape, float(PEER_TOPK), F32)
    for r in range(PEER_TOPK):
        m = jnp.max(work, axis=0, keepdims=True)
        v_ref[r:r + 1, :] = m
        hit = work == m
        rank = jnp.where(hit, float(r), rank)
        work = jnp.where(hit, -jnp.inf, work)
    return rank


def _pair_select(v1, v2):
    row = lax.broadcasted_iota(jnp.int32, (8, LANES), 0)
    neg = -jnp.inf
    cands = [v1[0:1] + v2[0:8], v1[0:1] + v2[8:16], v1[1:2] + v2[0:8]]
    for a in range(2, 8):
        cands.append(jnp.where(row < PEER_TOPK // (a + 1), v1[a:a + 1] + v2[0:8], neg))
    cands.append(v1[8:16] + v2[0:1])

    work = list(cands)
    vals = []
    for _ in range(PEER_TOPK):
        m8 = work[0]
        for w in work[1:]:
            m8 = jnp.maximum(m8, w)
        m = jnp.max(m8, axis=0, keepdims=True)
        vals.append(m)
        work = [jnp.where(w == m, neg, w) for w in work]
    tau, top = vals[-1], vals[0]
    z = jnp.ones_like(top)
    for v in vals[1:]:
        z = z + jnp.exp(v - top)

    sel = [(c >= tau).astype(F32) for c in cands]
    colsum = lambda a: jnp.sum(a, axis=0, keepdims=True)
    n_lo = [colsum(sel[0]) + colsum(sel[1])] + [colsum(sel[i]) for i in range(2, 9)]
    n_hi = sel[9]
    return n_lo, n_hi, z


def _route_kernel(x_ref, g_ref, wq_ref, k1_ref, k2_ref,
                  xnt_ref, re_ref, n1_ref, e1_ref, qt_ref, s_ref, v1_ref, v2_ref, *, tt):
    xt = _rmsnorm(x_ref[...], g_ref[...]).T.astype(BF16)
    xnt_ref[...] = xt
    qt_ref[...] = _dot(wq_ref[...], xt)

    def head_body(h, carry):
        q = qt_ref[pl.ds(pl.multiple_of(h * (2 * PEER_HALF), 2 * PEER_HALF), 2 * PEER_HALF), :]
        s1 = _dot(k1_ref[h], q[:PEER_HALF].astype(BF16))
        s2 = _dot(k2_ref[h], q[PEER_HALF:].astype(BF16))
        for c in range(tt // LANES):
            s_ref[0, c] = s1[:, c * LANES:(c + 1) * LANES]
            s_ref[1, c] = s2[:, c * LANES:(c + 1) * LANES]
        for c in range(tt // LANES):
            cols = slice(c * LANES, (c + 1) * LANES)
            rank1 = _top16(s_ref[0, c], v1_ref)
            rank2 = _top16(s_ref[1, c], v2_ref)
            v1 = v1_ref[...]
            v2 = v2_ref[...]
            r2 = (rank2 * RANK_SCALE).astype(BF16)
            e2 = jnp.exp(s_ref[1, c] - v2[0:1]).astype(BF16)
            for sub in range(PEER_NKEYS // BF16_SUBLANES):
                rows = slice(sub * BF16_SUBLANES, (sub + 1) * BF16_SUBLANES)
                re_ref[c, sub, h, 0] = r2[rows]
                re_ref[c, sub, h, 1] = e2[rows]
            n_lo, n_hi, z = _pair_select(v1, v2)
            n1 = jnp.zeros((PEER_NKEYS, LANES), F32)
            for a in range(PEER_TOPK):
                n_a = n_lo[a] if a < 8 else n_hi[a - 8:a - 7]
                n1 = jnp.where(rank1 == float(a), n_a, n1)
            n1_ref[h, :, cols] = n1 * RANK_SCALE + (1.0 - RANK_SCALE)
            e1_ref[h, :, cols] = jnp.exp(s_ref[0, c] - v1[0:1]) * (1.0 / z)
        return carry

    lax.fori_loop(0, PEER_HEADS, head_body, 0)


def _re_shape(n_tokens):
    return (n_tokens // LANES, PEER_NKEYS // BF16_SUBLANES, PEER_HEADS, 2, BF16_SUBLANES, LANES)


def _peer_route(x, g, wq, k1, k2):
    T, D = x.shape
    tt = min(T, 512)
    hd = wq.shape[0]
    kern = functools.partial(_route_kernel, tt=tt)
    c2 = lambda i: (0, 0)
    c3 = lambda i: (0, 0, 0)
    tabf = jax.ShapeDtypeStruct((PEER_HEADS, PEER_NKEYS, T), F32)
    tab_spec = pl.BlockSpec((PEER_HEADS, PEER_NKEYS, tt), lambda i: (0, 0, i))
    re_shape = _re_shape(T)
    re_spec = pl.BlockSpec(_re_shape(tt), lambda i: (i, 0, 0, 0, 0, 0))
    return pl.pallas_call(
        kern,
        grid=(T // tt,),
        in_specs=[
            pl.BlockSpec((tt, D), lambda i: (i, 0)),
            pl.BlockSpec((1, D), c2),
            pl.BlockSpec((hd, D), c2),
            pl.BlockSpec(k1.shape, c3),
            pl.BlockSpec(k2.shape, c3),
        ],
        out_specs=[pl.BlockSpec((D, tt), lambda i: (0, i)), re_spec, tab_spec, tab_spec],
        out_shape=[jax.ShapeDtypeStruct((D, T), BF16), jax.ShapeDtypeStruct(re_shape, BF16), tabf, tabf],
        scratch_shapes=[
            pltpu.VMEM((hd, tt), F32),
            pltpu.VMEM((2, tt // LANES, PEER_NKEYS, LANES), F32),
            pltpu.VMEM((PEER_TOPK, LANES), F32),
            pltpu.VMEM((PEER_TOPK, LANES), F32),
        ],
        compiler_params=pltpu.CompilerParams(
            dimension_semantics=("arbitrary",), vmem_limit_bytes=VMEM_LIMIT),
        name="peer_route",
    )(x, g.reshape(1, D), wq, k1, k2)


def _gelu(h):
    return 0.5 * h * (1.0 + lax.erf(h * (1.0 / math.sqrt(2.0))))


def _row_bf16(ref, h, k, cols):
    return jnp.broadcast_to(ref[h, 0, k:k + 1, cols], (BF16_SUBLANES, LANES)).astype(BF16)


def _expert_step(xnt_ref, u_ref, vt_ref, re_ref, n1_ref, e1_ref, acc_ref, w_ref, hid_w, hid_r, *, tt, ec):
    tw = min(tt, MXU_DIM)
    kpt = MXU_DIM // PEER_NKEYS

    def token_tile(n, carry):
        t0 = pl.multiple_of(n * tw, tw)
        tok = pl.ds(t0, tw)
        hid_w[:, tok] = _dot(u_ref[...], xnt_ref[:, tok])

        part = None
        for kt in range(ec // MXU_DIM):
            ks = range(kt * kpt, (kt + 1) * kpt)
            for cc in range(tw // LANES):
                c = n * (tw // LANES) + cc
                cols = pl.ds(pl.multiple_of(t0 + cc * LANES, LANES), LANES)
                wcols = slice(cc * LANES, (cc + 1) * LANES)
                n1 = [[_row_bf16(n1_ref, h, k, cols) for h in range(PEER_HEADS)] for k in ks]
                e1 = [[_row_bf16(e1_ref, h, k, cols) for h in range(PEER_HEADS)] for k in ks]
                for sub in range(PEER_NKEYS // BF16_SUBLANES):
                    gates = [None] * kpt
                    for h in range(PEER_HEADS):
                        r2 = re_ref[c, sub, h, 0]
                        e2 = re_ref[c, sub, h, 1]
                        for q in range(kpt):
                            g_h = (e1[q][h] * jnp.floor(n1[q][h] - r2)) * e2
                            gates[q] = g_h if gates[q] is None else gates[q] + g_h
                    for q, k in enumerate(ks):
                        r0 = k * PEER_NKEYS + sub * BF16_SUBLANES
                        rows = slice(r0, r0 + BF16_SUBLANES)
                        w_ref[rows, wcols] = gates[q] * _gelu(hid_r[rows, cols]).astype(BF16)
            kk = slice(kt * MXU_DIM, (kt + 1) * MXU_DIM)
            p = _dot(vt_ref[0, :, kk], w_ref[kk, :])
            part = p if part is None else part + p
        acc_ref[:, tok] += part
        return carry

    lax.fori_loop(0, tt // tw, token_tile, 0)


def _expert_kernel(xnt_ref, x_ref, u_ref, vt_ref, re_ref, n1_ref, e1_ref, gf_ref,
                   o_ref, acc_ref, hid_0, hid_1, w_ref, *, tt, ec, n_j, final):
    g = pl.program_id(0)
    b = g - 1
    jb = jnp.maximum(b, 0) % n_j
    step = functools.partial(_expert_step, xnt_ref, u_ref, vt_ref, re_ref, n1_ref, e1_ref, acc_ref, w_ref,
                             tt=tt, ec=ec)

    @pl.when(g == 0)
    def _():
        hid_1[...] = jnp.zeros_like(hid_1)

    @pl.when(jb == 0)
    def _():
        acc_ref[...] = jnp.zeros_like(acc_ref)

    @pl.when(g % 2 == 0)
    def _():
        step(hid_0, hid_1)

    @pl.when(g % 2 == 1)
    def _():
        step(hid_1, hid_0)

    @pl.when((jb == n_j - 1) & (b >= 0))
    def _():
        out = x_ref[...] + acc_ref[...].T
        if final:
            out = _rmsnorm(out, gf_ref[...])
        o_ref[...] = out


def _peer_experts(x, xnt, u_bf, vt_bf, re, n1, e1, g_final, final):
    T, D = x.shape
    n_exp = u_bf.shape[0]
    tt = min(T, 512)
    ec = EXPERT_CHUNK
    n_j = n_exp // ec
    assert n_j > 1 and ec % MXU_DIM == 0
    i1c = ec // PEER_NKEYS
    n1 = n1.reshape(PEER_HEADS, PEER_NKEYS // i1c, i1c, T)
    e1 = e1.reshape(PEER_HEADS, PEER_NKEYS // i1c, i1c, T)
    n_flat = (T // tt) * n_j
    kern = functools.partial(_expert_kernel, tt=tt, ec=ec, n_j=n_j, final=final)

    def stage(lag):
        flat = lambda g: jnp.clip(g - lag, 0, n_flat - 1)
        return (lambda g: flat(g) // n_j), (lambda g: flat(g) % n_j)

    tile_a, chunk_a = stage(0)
    tile_b, chunk_b = stage(1)
    row_spec = pl.BlockSpec((PEER_HEADS, 1, i1c, tt), lambda g: (0, chunk_b(g), 0, tile_b(g)))
    return pl.pallas_call(
        kern,
        grid=(n_flat + 1,),
        in_specs=[
            pl.BlockSpec((D, tt), lambda g: (0, tile_a(g))),
            pl.BlockSpec((tt, D), lambda g: (tile_b(g), 0)),
            pl.BlockSpec((ec, D), lambda g: (chunk_a(g), 0)),
            pl.BlockSpec((1, D, ec), lambda g: (chunk_b(g), 0, 0)),
            pl.BlockSpec(_re_shape(tt), lambda g: (tile_b(g), 0, 0, 0, 0, 0)),
            row_spec, row_spec,
            pl.BlockSpec((1, D), lambda g: (0, 0)),
        ],
        out_specs=pl.BlockSpec((tt, D), lambda g: (tile_b(g), 0)),
        out_shape=jax.ShapeDtypeStruct((T, D), F32),
        scratch_shapes=[
            pltpu.VMEM((D, tt), F32),
            pltpu.VMEM((ec, tt), F32),
            pltpu.VMEM((ec, tt), F32),
            pltpu.VMEM((ec, min(tt, MXU_DIM)), BF16),
        ],
        compiler_params=pltpu.CompilerParams(
            dimension_semantics=("arbitrary",), vmem_limit_bytes=VMEM_LIMIT),
        name="peer_experts",
    )(xnt, x, u_bf, vt_bf, re, n1, e1, g_final.reshape(1, D))


def _split_param(a):
    hi = a.astype(BF16)
    return hi, (a - hi.astype(F32)).astype(BF16)


def _trunk(x, pool_hist, conv_hist, pos0, p):
    B, S, D = x.shape
    x, new_pool = _pool_layer(x, pool_hist, pos0, p["norm_mix"][0], p["pool_w_hi"], p["pool_w_lo"], p["pool_scale"][0])
    for i in range(2):
        if i == 1:
            x, new_conv = _conv_layer(x, conv_hist, p["norm_mix"][1], p["conv_w_in"][0], p["conv_b_in"][0],
                                      p["conv_dw"][0], p["conv_dw_b"][0], p["conv_ln_g"][0], p["conv_ln_b"][0],
                                      p["conv_w_out"][0], p["conv_b_out"][0])
        xt = x.reshape(B * S, D)
        xnt, re, n1, e1 = _peer_route(xt, p["norm_ffn"][i], *p["route"][i])
        xt = _peer_experts(xt, xnt, p["u_bf"][i], p["vt_bf"][i], re, n1, e1, p["norm_final"], final=(i == 1))
        x = xt.reshape(B, S, D)
    return x, new_pool[None], new_conv[None]


def kernel(x_prompt, x_sample, cache_pool, state_conv, norm_mix, norm_ffn, norm_final, pool_w, pool_scale, conv_w_in, conv_b_in, conv_dw, conv_dw_b, conv_ln_g, conv_ln_b, conv_w_out, conv_b_out, peer_wq, peer_keys, peer_u, peer_v):
    B, _, D = x_prompt.shape
    depth = norm_mix.shape[0]
    assert depth == 2 and pool_w.shape[0] == 1 and conv_w_in.shape[0] == 1
    assert peer_keys.shape[1:] == (PEER_HEADS, 2, PEER_NKEYS, PEER_HALF)

    pool_w_hi, pool_w_lo = _split_param(pool_w[0])
    route = []
    for i in range(depth):
        route.append((peer_wq[i].T.astype(BF16), peer_keys[i, :, 0].astype(BF16), peer_keys[i, :, 1].astype(BF16)))
    p = dict(
        norm_mix=norm_mix, norm_ffn=norm_ffn, norm_final=norm_final,
        pool_w_hi=pool_w_hi, pool_w_lo=pool_w_lo, pool_scale=pool_scale,
        conv_w_in=conv_w_in, conv_b_in=conv_b_in, conv_dw=conv_dw, conv_dw_b=conv_dw_b,
        conv_ln_g=conv_ln_g, conv_ln_b=conv_ln_b, conv_w_out=conv_w_out, conv_b_out=conv_b_out,
        route=route,
        u_bf=[peer_u[i].astype(BF16) for i in range(depth)],
        vt_bf=[peer_v[i].astype(BF16).reshape(-1, EXPERT_CHUNK, D).transpose(0, 2, 1) for i in range(depth)],
    )
    zero_pool = jnp.zeros((B, POOL_HIST, D), x_prompt.dtype)
    zero_conv = jnp.zeros((B, CONV_HIST, D), x_prompt.dtype)
    y_p, np_p, nc_p = _trunk(x_prompt, zero_pool, zero_conv, 0, p)
    past_len = 4096
    y_s, np_s, nc_s = _trunk(x_sample, cache_pool[0], state_conv[0], past_len, p)
    return (y_p, y_s, np_p, nc_p, np_s, nc_s)
```

```python
import functools
import math

import jax
import jax.numpy as jnp
from jax import lax
from jax.experimental import pallas as pl
from jax.experimental.pallas import tpu as pltpu

EPS = 1e-6
POOL_WINDOWS = (2, 4, 8, 16)
POOL_HIST = 15
POOL_PAD = 16
CONV_WIDTH = 31
CONV_HIST = 30
CONV_PAD = 32
PEER_HEADS = 8
PEER_NKEYS = 128
PEER_HALF = 64
PEER_TOPK = 16
RANK_SCALE = 1.0 / 32.0
TAKEN = 2.0 ** 100
TAKEN_STEP = 2.0 ** -20
LANES = 128
SUBLANES = 8
BF16_SUBLANES = 16
MXU_DIM = 256
EXPERT_CHUNK = 1024
VMEM_LIMIT = 56 * 1024 * 1024

F32 = jnp.float32
BF16 = jnp.bfloat16


def _rmsnorm(x, g):
    ms = jnp.mean(x * x, axis=-1, keepdims=True)
    return x * lax.rsqrt(ms + EPS) * g


def _split_bf16(a):
    hi = a.astype(BF16)
    lo = (a - hi.astype(F32)).astype(BF16)
    return hi, lo


def _dot(a, b):
    return jnp.dot(a, b, preferred_element_type=F32)


def _dot3(a_hi, a_lo, b_hi, b_lo):
    return _dot(a_hi, b_hi) + _dot(a_lo, b_hi) + _dot(a_hi, b_lo)


def _pool_kernel(x_ref, hist_ref, g_ref, wh_ref, wl_ref, sc_ref, o_ref, np_ref, xp_ref, *, ts, pos0, n_s, gw):
    s = pl.program_id(1)

    @pl.when(s == 0)
    def _():
        xp_ref[0:POOL_PAD, :] = hist_ref[0]

    @pl.when(s > 0)
    def _():
        xp_ref[0:POOL_PAD, :] = xp_ref[ts:ts + POOL_PAD, :]

    x = x_ref[0]
    xn = _rmsnorm(x, g_ref[...])
    xp_ref[POOL_PAD:POOL_PAD + ts, :] = xn

    for g, win in enumerate(POOL_WINDOWS):
        c0 = g * gw
        xg = xn[:, c0:c0 + gw]
        acc = xg
        for j in range(1, win):
            acc = acc + xp_ref[POOL_PAD - j:POOL_PAD - j + ts, c0:c0 + gw]
        if pos0 >= win - 1:
            mean = acc * (1.0 / win)
        else:
            pos = pos0 + s * ts + lax.broadcasted_iota(jnp.int32, (ts, gw), 0)
            cnt = jnp.minimum(win, pos + 1).astype(F32)
            mean = acc / cnt
        d_hi, d_lo = _split_bf16(mean - xg)
        y = _dot3(d_hi, d_lo, wh_ref[g], wl_ref[g])
        o_ref[0, :, c0:c0 + gw] = x[:, c0:c0 + gw] + y * sc_ref[:, c0:c0 + gw]

    @pl.when(s == n_s - 1)
    def _():
        np_ref[0] = xp_ref[ts:ts + POOL_PAD, :]


def _pool_layer(x, hist, pos0, g, w_hi, w_lo, scale):
    B, S, D = x.shape
    ts = min(S, 512)
    n_s = S // ts
    gw = D // len(POOL_WINDOWS)
    hist_p = jnp.pad(hist, ((0, 0), (POOL_PAD - POOL_HIST, 0), (0, 0)))
    kern = functools.partial(_pool_kernel, ts=ts, pos0=pos0, n_s=n_s, gw=gw)
    out, new_hist = pl.pallas_call(
        kern,
        grid=(B, n_s),
        in_specs=[
            pl.BlockSpec((1, ts, D), lambda b, s: (b, s, 0)),
            pl.BlockSpec((1, POOL_PAD, D), lambda b, s: (b, 0, 0)),
            pl.BlockSpec((1, D), lambda b, s: (0, 0)),
            pl.BlockSpec(w_hi.shape, lambda b, s: (0, 0, 0)),
            pl.BlockSpec(w_lo.shape, lambda b, s: (0, 0, 0)),
            pl.BlockSpec((1, D), lambda b, s: (0, 0)),
        ],
        out_specs=[
            pl.BlockSpec((1, ts, D), lambda b, s: (b, s, 0)),
            pl.BlockSpec((1, POOL_PAD, D), lambda b, s: (b, 0, 0)),
        ],
        out_shape=[
            jax.ShapeDtypeStruct((B, S, D), F32),
            jax.ShapeDtypeStruct((B, POOL_PAD, D), F32),
        ],
        scratch_shapes=[pltpu.VMEM((POOL_PAD + ts, D), F32)],
        compiler_params=pltpu.CompilerParams(
            dimension_semantics=("arbitrary", "arbitrary"), vmem_limit_bytes=VMEM_LIMIT),
        name="pool_layer",
    )(x, hist_p, g.reshape(1, D), w_hi, w_lo, scale.reshape(1, D))
    return out, new_hist[:, POOL_PAD - POOL_HIST:]


def _conv_kernel(x_ref, hist_ref, g_ref, win_ref, bin_ref, dw_ref, dwb_ref, lng_ref, lnb_ref, wout_ref, bout_ref,
                 o_ref, nc_ref, up_ref, z_ref, *, ts, n_s, rb):
    s = pl.program_id(1)
    D = x_ref.shape[-1]

    @pl.when(s == 0)
    def _():
        up_ref[0:CONV_PAD, :] = hist_ref[0]

    @pl.when(s > 0)
    def _():
        up_ref[0:CONV_PAD, :] = up_ref[ts:ts + CONV_PAD, :]

    x = x_ref[0]
    xn = _rmsnorm(x, g_ref[...])
    h = _dot(xn.astype(BF16), win_ref[...]) + bin_ref[...]
    u = h[:, :D] * jax.nn.sigmoid(h[:, D:])
    up_ref[CONV_PAD:CONV_PAD + ts, :] = u

    base = CONV_PAD - CONV_HIST
    for r0 in range(0, ts, rb):
        for c0 in range(0, D, LANES):
            cols = slice(c0, c0 + LANES)
            win = up_ref[r0:r0 + rb + CONV_PAD, cols]
            acc = jnp.zeros((rb, LANES), F32) + dwb_ref[:, cols]
            for shift in range(SUBLANES):
                taps = [k for k in range(CONV_WIDTH) if (base + k) % SUBLANES == shift]
                reach = max(base + k - shift for k in taps) + rb
                xs = win[shift:shift + reach]
                for k in taps:
                    a = base + k - shift
                    acc = acc + xs[a:a + rb] * dw_ref[k:k + 1, cols]
            z_ref[r0:r0 + rb, cols] = acc

    z = z_ref[...]
    mu = jnp.mean(z, axis=-1, keepdims=True)
    zc = z - mu
    var = jnp.mean(zc * zc, axis=-1, keepdims=True)
    zn = zc * lax.rsqrt(var + EPS) * lng_ref[...] + lnb_ref[...]
    act = zn * jax.nn.sigmoid(zn)
    y = _dot(act.astype(BF16), wout_ref[...]) + bout_ref[...]
    o_ref[0] = x + y

    @pl.when(s == n_s - 1)
    def _():
        nc_ref[0] = up_ref[ts:ts + CONV_PAD, :]


def _conv_layer(x, hist, g, w_in, b_in, dw, dw_b, ln_g, ln_b, w_out, b_out):
    B, S, D = x.shape
    ts = min(S, 256)
    n_s = S // ts
    hist_p = jnp.pad(hist, ((0, 0), (CONV_PAD - CONV_HIST, 0), (0, 0)))
    kern = functools.partial(_conv_kernel, ts=ts, n_s=n_s, rb=min(ts, 64))
    row = lambda a: a.reshape(1, -1)
    const2 = lambda b, s: (0, 0)
    out, new_hist = pl.pallas_call(
        kern,
        grid=(B, n_s),
        in_specs=[
            pl.BlockSpec((1, ts, D), lambda b, s: (b, s, 0)),
            pl.BlockSpec((1, CONV_PAD, D), lambda b, s: (b, 0, 0)),
            pl.BlockSpec((1, D), const2),
            pl.BlockSpec((D, 2 * D), const2),
            pl.BlockSpec((1, 2 * D), const2),
            pl.BlockSpec((CONV_WIDTH, D), const2),
            pl.BlockSpec((1, D), const2),
            pl.BlockSpec((1, D), const2),
            pl.BlockSpec((1, D), const2),
            pl.BlockSpec((D, D), const2),
            pl.BlockSpec((1, D), const2),
        ],
        out_specs=[
            pl.BlockSpec((1, ts, D), lambda b, s: (b, s, 0)),
            pl.BlockSpec((1, CONV_PAD, D), lambda b, s: (b, 0, 0)),
        ],
        out_shape=[
            jax.ShapeDtypeStruct((B, S, D), F32),
            jax.ShapeDtypeStruct((B, CONV_PAD, D), F32),
        ],
        scratch_shapes=[pltpu.VMEM((CONV_PAD + ts, D), F32), pltpu.VMEM((ts, D), F32)],
        compiler_params=pltpu.CompilerParams(
            dimension_semantics=("arbitrary", "arbitrary"), vmem_limit_bytes=VMEM_LIMIT),
        name="conv_layer",
    )(x, hist_p, row(g), w_in.astype(BF16), row(b_in), dw, row(dw_b), row(ln_g), row(ln_b),
      w_out.astype(BF16), row(b_out))
    return out, new_hist[:, CONV_PAD - CONV_HIST:]


def _top16(s, v_ref):
    work = s
    for r in range(PEER_TOPK):
        m = jnp.max(work, axis=0, keepdims=True)
        v_ref[r:r + 1, :] = m
        work = jnp.where(work == m, -TAKEN * (1.0 + r * TAKEN_STEP), work)
    taken = work <= -TAKEN
    return jnp.where(taken, (work * (-1.0 / TAKEN) - 1.0) * (1.0 / TAKEN_STEP), float(PEER_TOPK))


def _pair_select(v1, v2):
    row = lax.broadcasted_iota(jnp.int32, (8, LANES), 0)
    neg = -jnp.inf
    cands = [v1[0:1] + v2[0:8], v1[0:1] + v2[8:16], v1[1:2] + v2[0:8]]
    for a in range(2, 8):
        cands.append(jnp.where(row < PEER_TOPK // (a + 1), v1[a:a + 1] + v2[0:8], neg))
    cands.append(v1[8:16] + v2[0:1])

    work = list(cands)
    vals = []
    for _ in range(PEER_TOPK):
        m8 = work[0]
        for w in work[1:]:
            m8 = jnp.maximum(m8, w)
        m = jnp.max(m8, axis=0, keepdims=True)
        vals.append(m)
        work = [jnp.where(w == m, neg, w) for w in work]
    tau, top = vals[-1], vals[0]
    z = jnp.ones_like(top)
    for v in vals[1:]:
        z = z + jnp.exp(v - top)

    sel = [(c >= tau).astype(F32) for c in cands]
    colsum = lambda a: jnp.sum(a, axis=0, keepdims=True)
    n_lo = [colsum(sel[0]) + colsum(sel[1])] + [colsum(sel[i]) for i in range(2, 9)]
    n_hi = sel[9]
    return n_lo, n_hi, z


def _route_kernel(x_ref, g_ref, wq_ref, k1_ref, k2_ref,
                  xnt_ref, re_ref, n1_ref, e1_ref, qt_ref, s_ref, v1_ref, v2_ref, *, tt):
    xt = _rmsnorm(x_ref[...], g_ref[...]).T.astype(BF16)
    xnt_ref[...] = xt
    qt_ref[...] = _dot(wq_ref[...], xt)

    def head_body(h, carry):
        q = qt_ref[pl.ds(pl.multiple_of(h * (2 * PEER_HALF), 2 * PEER_HALF), 2 * PEER_HALF), :]
        s1 = _dot(k1_ref[h], q[:PEER_HALF].astype(BF16))
        s2 = _dot(k2_ref[h], q[PEER_HALF:].astype(BF16))
        for c in range(tt // LANES):
            s_ref[0, c] = s1[:, c * LANES:(c + 1) * LANES]
            s_ref[1, c] = s2[:, c * LANES:(c + 1) * LANES]
        for c in range(tt // LANES):
            cols = slice(c * LANES, (c + 1) * LANES)
            rank1 = _top16(s_ref[0, c], v1_ref)
            rank2 = _top16(s_ref[1, c], v2_ref)
            v1 = v1_ref[...]
            v2 = v2_ref[...]
            r2 = (rank2 * RANK_SCALE).astype(BF16)
            e2 = jnp.exp(s_ref[1, c] - v2[0:1]).astype(BF16)
            for sub in range(PEER_NKEYS // BF16_SUBLANES):
                rows = slice(sub * BF16_SUBLANES, (sub + 1) * BF16_SUBLANES)
                re_ref[c, sub, h, 0] = r2[rows]
                re_ref[c, sub, h, 1] = e2[rows]
            n_lo, n_hi, z = _pair_select(v1, v2)
            n1 = jnp.zeros((PEER_NKEYS, LANES), F32)
            for a in range(PEER_TOPK):
                n_a = n_lo[a] if a < 8 else n_hi[a - 8:a - 7]
                n1 = jnp.where(rank1 == float(a), n_a, n1)
            n1_ref[h, :, cols] = n1 * RANK_SCALE + (1.0 - RANK_SCALE)
            e1_ref[h, :, cols] = jnp.exp(s_ref[0, c] - v1[0:1]) * (1.0 / z)
        return carry

    lax.fori_loop(0, PEER_HEADS, head_body, 0)


def _re_shape(n_tokens):
    return (n_tokens // LANES, PEER_NKEYS // BF16_SUBLANES, PEER_HEADS, 2, BF16_SUBLANES, LANES)


def _peer_route(x, g, wq, k1, k2):
    T, D = x.shape
    tt = min(T, 512)
    hd = wq.shape[0]
    kern = functools.partial(_route_kernel, tt=tt)
    c2 = lambda i: (0, 0)
    c3 = lambda i: (0, 0, 0)
    tabf = jax.ShapeDtypeStruct((PEER_HEADS, PEER_NKEYS, T), F32)
    tab_spec = pl.BlockSpec((PEER_HEADS, PEER_NKEYS, tt), lambda i: (0, 0, i))
    re_shape = _re_shape(T)
    re_spec = pl.BlockSpec(_re_shape(tt), lambda i: (i, 0, 0, 0, 0, 0))
    return pl.pallas_call(
        kern,
        grid=(T // tt,),
        in_specs=[
            pl.BlockSpec((tt, D), lambda i: (i, 0)),
            pl.BlockSpec((1, D), c2),
            pl.BlockSpec((hd, D), c2),
            pl.BlockSpec(k1.shape, c3),
            pl.BlockSpec(k2.shape, c3),
        ],
        out_specs=[pl.BlockSpec((D, tt), lambda i: (0, i)), re_spec, tab_spec, tab_spec],
        out_shape=[jax.ShapeDtypeStruct((D, T), BF16), jax.ShapeDtypeStruct(re_shape, BF16), tabf, tabf],
        scratch_shapes=[
            pltpu.VMEM((hd, tt), F32),
            pltpu.VMEM((2, tt // LANES, PEER_NKEYS, LANES), F32),
            pltpu.VMEM((PEER_TOPK, LANES), F32),
            pltpu.VMEM((PEER_TOPK, LANES), F32),
        ],
        compiler_params=pltpu.CompilerParams(
            dimension_semantics=("arbitrary",), vmem_limit_bytes=VMEM_LIMIT),
        name="peer_route",
    )(x, g.reshape(1, D), wq, k1, k2)


def _gelu(h):
    return 0.5 * h * (1.0 + lax.erf(h * (1.0 / math.sqrt(2.0))))


def _row_bf16(ref, h, k, cols):
    return jnp.broadcast_to(ref[h, 0, k:k + 1, cols], (BF16_SUBLANES, LANES)).astype(BF16)


def _expert_step(xnt_ref, u_ref, vt_ref, re_ref, n1_ref, e1_ref, acc_ref, w_ref, hid_w, hid_r, *, tt, ec):
    tw = min(tt, MXU_DIM)
    kpt = MXU_DIM // PEER_NKEYS

    def token_tile(n, carry):
        t0 = pl.multiple_of(n * tw, tw)
        tok = pl.ds(t0, tw)
        hid_w[:, tok] = _dot(u_ref[...], xnt_ref[:, tok])

        part = None
        for kt in range(ec // MXU_DIM):
            ks = range(kt * kpt, (kt + 1) * kpt)
            for cc in range(tw // LANES):
                c = n * (tw // LANES) + cc
                cols = pl.ds(pl.multiple_of(t0 + cc * LANES, LANES), LANES)
                wcols = slice(cc * LANES, (cc + 1) * LANES)
                n1 = [[_row_bf16(n1_ref, h, k, cols) for h in range(PEER_HEADS)] for k in ks]
                e1 = [[_row_bf16(e1_ref, h, k, cols) for h in range(PEER_HEADS)] for k in ks]
                for sub in range(PEER_NKEYS // BF16_SUBLANES):
                    gates = [None] * kpt
                    for h in range(PEER_HEADS):
                        r2 = re_ref[c, sub, h, 0]
                        e2 = re_ref[c, sub, h, 1]
                        for q in range(kpt):
                            g_h = (e1[q][h] * jnp.floor(n1[q][h] - r2)) * e2
                            gates[q] = g_h if gates[q] is None else gates[q] + g_h
                    for q, k in enumerate(ks):
                        r0 = k * PEER_NKEYS + sub * BF16_SUBLANES
                        rows = slice(r0, r0 + BF16_SUBLANES)
                        w_ref[rows, wcols] = gates[q] * _gelu(hid_r[rows, cols]).astype(BF16)
            kk = slice(kt * MXU_DIM, (kt + 1) * MXU_DIM)
            p = _dot(vt_ref[0, :, kk], w_ref[kk, :])
            part = p if part is None else part + p
        acc_ref[:, tok] += part
        return carry

    lax.fori_loop(0, tt // tw, token_tile, 0)


def _expert_kernel(xnt_ref, x_ref, u_ref, vt_ref, re_ref, n1_ref, e1_ref, gf_ref,
                   o_ref, acc_ref, hid_0, hid_1, w_ref, *, tt, ec, n_j, final):
    g = pl.program_id(0)
    b = g - 1
    jb = jnp.maximum(b, 0) % n_j
    step = functools.partial(_expert_step, xnt_ref, u_ref, vt_ref, re_ref, n1_ref, e1_ref, acc_ref, w_ref,
                             tt=tt, ec=ec)

    @pl.when(g == 0)
    def _():
        hid_1[...] = jnp.zeros_like(hid_1)

    @pl.when(jb == 0)
    def _():
        acc_ref[...] = jnp.zeros_like(acc_ref)

    @pl.when(g % 2 == 0)
    def _():
        step(hid_0, hid_1)

    @pl.when(g % 2 == 1)
    def _():
        step(hid_1, hid_0)

    @pl.when((jb == n_j - 1) & (b >= 0))
    def _():
        out = x_ref[...] + acc_ref[...].T
        if final:
            out = _rmsnorm(out, gf_ref[...])
        o_ref[...] = out


def _peer_experts(x, xnt, u_bf, vt_bf, re, n1, e1, g_final, final):
    T, D = x.shape
    n_exp = u_bf.shape[0]
    tt = min(T, 512)
    ec = EXPERT_CHUNK
    n_j = n_exp // ec
    assert n_j > 1 and ec % MXU_DIM == 0
    i1c = ec // PEER_NKEYS
    n1 = n1.reshape(PEER_HEADS, PEER_NKEYS // i1c, i1c, T)
    e1 = e1.reshape(PEER_HEADS, PEER_NKEYS // i1c, i1c, T)
    n_flat = (T // tt) * n_j
    kern = functools.partial(_expert_kernel, tt=tt, ec=ec, n_j=n_j, final=final)

    def stage(lag):
        flat = lambda g: jnp.clip(g - lag, 0, n_flat - 1)
        return (lambda g: flat(g) // n_j), (lambda g: flat(g) % n_j)

    tile_a, chunk_a = stage(0)
    tile_b, chunk_b = stage(1)
    row_spec = pl.BlockSpec((PEER_HEADS, 1, i1c, tt), lambda g: (0, chunk_b(g), 0, tile_b(g)))
    return pl.pallas_call(
        kern,
        grid=(n_flat + 1,),
        in_specs=[
            pl.BlockSpec((D, tt), lambda g: (0, tile_a(g))),
            pl.BlockSpec((tt, D), lambda g: (tile_b(g), 0)),
            pl.BlockSpec((ec, D), lambda g: (chunk_a(g), 0)),
            pl.BlockSpec((1, D, ec), lambda g: (chunk_b(g), 0, 0)),
            pl.BlockSpec(_re_shape(tt), lambda g: (tile_b(g), 0, 0, 0, 0, 0)),
            row_spec, row_spec,
            pl.BlockSpec((1, D), lambda g: (0, 0)),
        ],
        out_specs=pl.BlockSpec((tt, D), lambda g: (tile_b(g), 0)),
        out_shape=jax.ShapeDtypeStruct((T, D), F32),
        scratch_shapes=[
            pltpu.VMEM((D, tt), F32),
            pltpu.VMEM((ec, tt), F32),
            pltpu.VMEM((ec, tt), F32),
            pltpu.VMEM((ec, min(tt, MXU_DIM)), BF16),
        ],
        compiler_params=pltpu.CompilerParams(
            dimension_semantics=("arbitrary",), vmem_limit_bytes=VMEM_LIMIT),
        name="peer_experts",
    )(xnt, x, u_bf, vt_bf, re, n1, e1, g_final.reshape(1, D))


def _split_param(a):
    hi = a.astype(BF16)
    return hi, (a - hi.astype(F32)).astype(BF16)


def _trunk(x, pool_hist, conv_hist, pos0, p):
    B, S, D = x.shape
    x, new_pool = _pool_layer(x, pool_hist, pos0, p["norm_mix"][0], p["pool_w_hi"], p["pool_w_lo"], p["pool_scale"][0])
    for i in range(2):
        if i == 1:
            x, new_conv = _conv_layer(x, conv_hist, p["norm_mix"][1], p["conv_w_in"][0], p["conv_b_in"][0],
                                      p["conv_dw"][0], p["conv_dw_b"][0], p["conv_ln_g"][0], p["conv_ln_b"][0],
                                      p["conv_w_out"][0], p["conv_b_out"][0])
        xt = x.reshape(B * S, D)
        xnt, re, n1, e1 = _peer_route(xt, p["norm_ffn"][i], *p["route"][i])
        xt = _peer_experts(xt, xnt, p["u_bf"][i], p["vt_bf"][i], re, n1, e1, p["norm_final"], final=(i == 1))
        x = xt.reshape(B, S, D)
    return x, new_pool[None], new_conv[None]


def kernel(x_prompt, x_sample, cache_pool, state_conv, norm_mix, norm_ffn, norm_final, pool_w, pool_scale, conv_w_in, conv_b_in, conv_dw, conv_dw_b, conv_ln_g, conv_ln_b, conv_w_out, conv_b_out, peer_wq, peer_keys, peer_u, peer_v):
    B, _, D = x_prompt.shape
    depth = norm_mix.shape[0]
    assert depth == 2 and pool_w.shape[0] == 1 and conv_w_in.shape[0] == 1
    assert peer_keys.shape[1:] == (PEER_HEADS, 2, PEER_NKEYS, PEER_HALF)

    pool_w_hi, pool_w_lo = _split_param(pool_w[0])
    route = []
    for i in range(depth):
        route.append((peer_wq[i].T.astype(BF16), peer_keys[i, :, 0].astype(BF16), peer_keys[i, :, 1].astype(BF16)))
    p = dict(
        norm_mix=norm_mix, norm_ffn=norm_ffn, norm_final=norm_final,
        pool_w_hi=pool_w_hi, pool_w_lo=pool_w_lo, pool_scale=pool_scale,
        conv_w_in=conv_w_in, conv_b_in=conv_b_in, conv_dw=conv_dw, conv_dw_b=conv_dw_b,
        conv_ln_g=conv_ln_g, conv_ln_b=conv_ln_b, conv_w_out=conv_w_out, conv_b_out=conv_b_out,
        route=route,
        u_bf=[peer_u[i].astype(BF16) for i in range(depth)],
        vt_bf=[peer_v[i].astype(BF16).reshape(-1, EXPERT_CHUNK, D).transpose(0, 2, 1) for i in range(depth)],
    )
    zero_pool = jnp.zeros((B, POOL_HIST, D), x_prompt.dtype)
    zero_conv = jnp.zeros((B, CONV_HIST, D), x_prompt.dtype)
    y_p, np_p, nc_p = _trunk(x_prompt, zero_pool, zero_conv, 0, p)
    past_len = 4096
    y_s, np_s, nc_s = _trunk(x_sample, cache_pool[0], state_conv[0], past_len, p)
    return (y_p, y_s, np_p, nc_p, np_s, nc_s)
```
